```python
import math, functools
import jax, jax.numpy as jnp
from jax import lax
import numpy as np

D_MODEL = 1024
BATCH = 2
SEQ = 8192
DEPTH = 2
DEC_BATCH = 128
DEC_SEQ = 8
PAST_LEN = 2048
PAGE_SIZE = 128

F32 = jnp.float32
ATTN_GROUPS = ((128, 1), (512, 4), (2048, 16))
N_GROUPS = 3
A_HEADS = 4
A_HEAD_DIM = 64
ROPE_THETA = 10000.0
BAND_BLOCK = 128
B_HEADS = 8
B_HEAD_DIM = 128
CONV_W = 4
CONV_CH = 3 * B_HEADS * B_HEAD_DIM
DELTA_CHUNK = 64
D_FF = 2816
N_EXPERTS = 8
TOP_K = 2
D_FF_EXPERT = 3584
N_DENSE = (DEPTH + 1) // 2
N_MOE = DEPTH // 2
NORM_EPS = 1e-6
A_QKV = N_GROUPS * 3 * A_HEADS * A_HEAD_DIM
B_Z = B_HEADS * B_HEAD_DIM
IN_SIZES = (A_QKV, CONV_CH, B_Z, B_HEADS, B_HEADS, D_MODEL, D_MODEL)
N_IN = A_QKV + CONV_CH + B_Z + 2 * B_HEADS + 2 * D_MODEL

kernel_name = 'hybrid_dilated_attn_gated_delta_adaln_step'


def rmsnorm(x, w):
    xf = x.astype(F32)
    y = xf * lax.rsqrt(jnp.mean(xf * xf, axis=-1, keepdims=True) + NORM_EPS)
    return (y * w.astype(F32)).astype(x.dtype)


def l2norm(x):
    xf = x.astype(F32)
    return xf * lax.rsqrt(jnp.sum(xf * xf, axis=-1, keepdims=True) + NORM_EPS)


def split_cols(t, sizes):
    out, o = [], 0
    for s in sizes:
        out.append(t[..., o:o + s])
        o += s
    return out


def rope(x, pos):
    half = x.shape[-1] // 2
    inv = jnp.exp(-math.log(ROPE_THETA) * jnp.arange(half, dtype=F32) / half)
    ang = pos.astype(F32)[:, None] * inv[None, :]
    shp = (1, pos.shape[0]) + (1,) * (x.ndim - 3) + (half,)
    cos, sin = jnp.cos(ang).reshape(shp), jnp.sin(ang).reshape(shp)
    xf = x.astype(F32)
    x1, x2 = xf[..., :half], xf[..., half:]
    return jnp.concatenate([x1 * cos - x2 * sin, x2 * cos + x1 * sin], axis=-1).astype(x.dtype)


def dilated_band_attention(q, k, v, window, dil):
    n, L, H, E = q.shape
    nk = window // dil
    R = -(-L // dil)
    Lr = -(-R // BAND_BLOCK) * BAND_BLOCK
    nb = Lr // BAND_BLOCK
    pad = Lr * dil - L

    def fold(t):
        t = jnp.pad(t, ((0, 0), (0, pad), (0, 0), (0, 0)))
        t = t.reshape(n, Lr, dil, H, E).transpose(0, 2, 3, 1, 4)
        return t.reshape(n, dil, H, nb, BAND_BLOCK, E)

    def with_prev(t):
        prev = jnp.pad(t[:, :, :, :-1], ((0, 0), (0, 0), (0, 0), (1, 0), (0, 0), (0, 0)))
        return jnp.concatenate([prev, t], axis=4)

    qb = fold(q)
    kw = with_prev(fold(k))
    vw = with_prev(fold(v))
    s = jnp.einsum('nrhbqe,nrhbke->nrhbqk', qb, kw, preferred_element_type=F32) * (E ** -0.5)
    qi = jnp.arange(BAND_BLOCK)[:, None]
    kj = jnp.arange(2 * BAND_BLOCK)[None, :]
    rel = kj - BAND_BLOCK - qi
    band = (rel <= 0) & (rel >= -nk)
    has_prev = (jnp.arange(nb)[:, None, None] > 0) | (kj[None] >= BAND_BLOCK)
    mask = band[None] & has_prev
    s = jnp.where(mask, s, -jnp.inf)
    lse = jax.nn.logsumexp(s, axis=-1)
    p = jnp.exp(s - lse[..., None])
    o = jnp.einsum('nrhbqk,nrhbke->nrhbqe', p.astype(v.dtype), vw, preferred_element_type=F32)
    o = o.reshape(n, dil, H, Lr, E).transpose(0, 3, 1, 2, 4).reshape(n, Lr * dil, H, E)[:, :L]
    lse = lse.reshape(n, dil, H, Lr).transpose(0, 3, 1, 2).reshape(n, Lr * dil, H)[:, :L]
    return o, lse


def dilated_cached_attention(q, k, v, k_buf, v_buf, window, dil):
    n, T, H, E = q.shape
    wb = k_buf.shape[1]
    nk = window // dil
    kk = jnp.concatenate([k_buf.astype(k.dtype), k], axis=1)
    vv = jnp.concatenate([v_buf.astype(v.dtype), v], axis=1)
    idx = wb + jnp.arange(T)[:, None] - dil * jnp.arange(nk + 1)[None, :]
    valid = idx >= 0
    idx = jnp.maximum(idx, 0)
    kg = kk[:, idx]
    vg = vv[:, idx]
    s = jnp.einsum('nthe,ntkhe->nthk', q, kg, preferred_element_type=F32) * (E ** -0.5)
    s = jnp.where(valid[None, :, None, :], s, -jnp.inf)
    lse = jax.nn.logsumexp(s, axis=-1)
    p = jnp.exp(s - lse[..., None])
    o = jnp.einsum('nthk,ntkhe->nthe', p.astype(v.dtype), vg, preferred_element_type=F32)
    return o, lse


def causal_conv(u, buf, w):
    uu = jnp.concatenate([buf.astype(u.dtype), u], axis=1)
    L = u.shape[1]
    out = uu[:, 0:L] * w[0]
    for i in range(1, CONV_W):
        out = out + uu[:, i:i + L] * w[i]
    return out, uu[:, -(CONV_W - 1):]


def gated_delta_chunked(q, k, v, beta, g, s0):
    n, L, H, dk = q.shape
    dv = v.shape[-1]
    C = DELTA_CHUNK
    nc = -(-L // C)
    pad = nc * C - L

    def chunk(t):
        t = jnp.pad(t, [(0, 0), (0, pad)] + [(0, 0)] * (t.ndim - 2))
        t = t.reshape((n, nc, C) + t.shape[2:])
        return t.transpose((1, 0, 3, 2) + tuple(range(4, t.ndim)))

    qc, kc, vc, bc, gc = chunk(q), chunk(k), chunk(v), chunk(beta), chunk(g)
    G = jnp.cumsum(gc, axis=-1)
    ii = jnp.arange(C)[:, None]
    jj = jnp.arange(C)[None, :]
    incl = ii >= jj
    strict = ii > jj
    diff = G[..., :, None] - G[..., None, :]
    dm = jnp.where(incl, jnp.exp(jnp.where(incl, diff, 0.0)), 0.0)
    kk = jnp.einsum('znhid,znhjd->znhij', kc, kc)
    a_mat = jnp.where(strict, bc[..., None] * kk * dm, 0.0) + jnp.eye(C, dtype=F32)
    rhs = jnp.concatenate([bc[..., None] * vc, (bc * jnp.exp(G))[..., None] * kc], axis=-1)
    sol = lax.linalg.triangular_solve(a_mat, rhs, left_side=True, lower=True, unit_diagonal=True)
    u_c, w_c = sol[..., :dv], sol[..., dv:]
    qk = jnp.einsum('znhid,znhjd->znhij', qc, kc) * dm
    qdec = qc * jnp.exp(G)[..., None]
    kdec = kc * jnp.exp(G[..., -1:] - G)[..., None]
    glast = jnp.exp(G[..., -1])

    def step(S, xs):
        u, w, qd, qkm, kd, gl = xs
        wv = u - jnp.einsum('nhcd,nhde->nhce', w, S)
        o = jnp.einsum('nhcd,nhde->nhce', qd, S) + jnp.einsum('nhij,nhje->nhie', qkm, wv)
        S = gl[..., None, None] * S + jnp.einsum('nhcd,nhce->nhde', kd, wv)
        return S, o

    S, o = lax.scan(step, s0, (u_c, w_c, qdec, qk, kdec, glast))
    o = o.transpose(1, 0, 3, 2, 4).reshape(n, nc * C, H, dv)[:, :L]
    return o, S


def hybrid_mixer(h, start_pos, kv_bufs, conv_buf, s0, w_in, conv_w, a_log, dt_bias, gdn_norm,
                 w_branch_a, w_branch_b, w_out):
    n, L, _ = h.shape
    proj = h @ w_in
    qkv_a, qkv_b, z, b_beta, b_a, gate_a, gate_b = split_cols(proj, IN_SIZES)

    pos = start_pos + jnp.arange(L)
    qkv_a = qkv_a.reshape(n, L, N_GROUPS, 3, A_HEADS, A_HEAD_DIM)
    q = rope(qkv_a[:, :, :, 0], pos)
    k = rope(qkv_a[:, :, :, 1], pos)
    v = qkv_a[:, :, :, 2]
    outs, lses, new_kv = [], [], []
    for gi, (win, dil) in enumerate(ATTN_GROUPS):
        qg, kg, vg = q[:, :, gi], k[:, :, gi], v[:, :, gi]
        if kv_bufs is None:
            o, lse = dilated_band_attention(qg, kg, vg, win, dil)
            keep = min(win, L)
            new_kv += [kg[:, L - keep:], vg[:, L - keep:]]
        else:
            o, lse = dilated_cached_attention(qg, kg, vg, kv_bufs[2 * gi], kv_bufs[2 * gi + 1], win, dil)
            new_kv += [kg, vg]
        outs.append(o)
        lses.append(lse)
    alpha = jax.nn.softmax(jnp.stack(lses), axis=0)
    o_a = jnp.sum(alpha[..., None] * jnp.stack(outs), axis=0).astype(h.dtype)
    y_a = o_a.reshape(n, L, A_HEADS * A_HEAD_DIM) @ w_branch_a

    if conv_buf is None:
        conv_buf = jnp.zeros((n, CONV_W - 1, CONV_CH), h.dtype)
    if s0 is None:
        s0 = jnp.zeros((n, B_HEADS, B_HEAD_DIM, B_HEAD_DIM), F32)
    cu, new_conv = causal_conv(qkv_b, conv_buf, conv_w)
    cu = jax.nn.silu(cu).reshape(n, L, 3, B_HEADS, B_HEAD_DIM)
    qb = l2norm(cu[:, :, 0]) * (B_HEAD_DIM ** -0.5)
    kb = l2norm(cu[:, :, 1])
    vb = cu[:, :, 2].astype(F32)
    beta = jax.nn.sigmoid(b_beta.astype(F32))
    g = -jnp.exp(a_log.astype(F32)) * jax.nn.softplus(b_a.astype(F32) + dt_bias.astype(F32))
    o_b, s_new = gated_delta_chunked(qb, kb, vb, beta, g, s0.astype(F32))
    zf = z.reshape(n, L, B_HEADS, B_HEAD_DIM).astype(F32)
    o_b = (rmsnorm(o_b, gdn_norm) * jax.nn.silu(zf)).astype(h.dtype)
    y_b = o_b.reshape(n, L, B_HEADS * B_HEAD_DIM) @ w_branch_b

    merged = jax.nn.sigmoid(gate_a) * y_a + jax.nn.sigmoid(gate_b) * y_b
    return merged @ w_out, new_kv, new_conv, s_new.astype(h.dtype)


def swiglu(h, w_up, w_down):
    gu = h @ w_up
    gte, up = jnp.split(gu, 2, axis=-1)
    return (jax.nn.silu(gte) * up) @ w_down


def moe(h, w_router, w_up, w_down):
    logits = jnp.einsum('nld,de->nle', h, w_router, preferred_element_type=F32)
    top_v, top_i = lax.top_k(logits, TOP_K)
    wts = jax.nn.softmax(top_v, axis=-1)
    comb = jnp.einsum('nlk,nlke->nle', wts, jax.nn.one_hot(top_i, N_EXPERTS, dtype=F32))
    y = jnp.zeros(h.shape, F32)
    for e in range(N_EXPERTS):
        y = y + comb[..., e:e + 1] * swiglu(h, w_up[e], w_down[e]).astype(F32)
    return y.astype(h.dtype)


def trunk_layer(x, c, start_pos, kv_bufs, conv_buf, s0, w_mod_l, b_mod_l, norm_mix_l, norm_ffn_l,
                mix_w, ffn):
    mod = (jax.nn.silu(c) @ w_mod_l + b_mod_l)[:, None, :]
    sh1, sc1, g1, sh2, sc2, g2 = jnp.split(mod, 6, axis=-1)
    h = rmsnorm(x, norm_mix_l) * (1 + sc1) + sh1
    mix, new_kv, new_conv, s_new = hybrid_mixer(h, start_pos, kv_bufs, conv_buf, s0, *mix_w)
    x = x + g1 * mix
    h = rmsnorm(x, norm_ffn_l) * (1 + sc2) + sh2
    x = x + g2 * ffn(h)
    return x, (*new_kv, new_conv, s_new)


def setup_inputs(seed: int = 0) -> dict:
    key = jax.random.key(seed)
    ks = iter(jax.random.split(key, 40))

    def nrm(shape, scale):
        return jax.random.normal(next(ks), shape, F32) * scale

    def gain(shape):
        return 1.0 + nrm(shape, 0.02)

    d = D_MODEL
    inp = {}
    inp['x_prompt'] = nrm((BATCH, SEQ, d), 1.0)
    inp['x_sample'] = nrm((DEC_BATCH, DEC_SEQ, d), 1.0)
    for gi, (win, dil) in enumerate(ATTN_GROUPS):
        rows = min(win, PAST_LEN)
        inp['cache_k_g%d' % gi] = nrm((DEPTH, DEC_BATCH, rows, A_HEADS, A_HEAD_DIM), 1.0)
        inp['cache_v_g%d' % gi] = nrm((DEPTH, DEC_BATCH, rows, A_HEADS, A_HEAD_DIM), 1.0)
    inp['state_conv'] = nrm((DEPTH, DEC_BATCH, CONV_W - 1, CONV_CH), 1.0)
    inp['state_delta'] = nrm((DEPTH, DEC_BATCH, B_HEADS, B_HEAD_DIM, B_HEAD_DIM), 0.1)
    inp['c_prompt'] = nrm((BATCH, d), 1.0)
    inp['c_sample'] = nrm((DEC_BATCH, d), 1.0)
    inp['w_mod'] = nrm((DEPTH, d, 6 * d), 0.5 * d ** -0.5)
    inp['b_mod'] = nrm((DEPTH, 6 * d), 0.1)
    inp['norm_mix'] = gain((DEPTH, d))
    inp['norm_ffn'] = gain((DEPTH, d))
    inp['w_in'] = nrm((DEPTH, d, N_IN), d ** -0.5)
    inp['conv_w'] = nrm((DEPTH, CONV_W, CONV_CH), 0.5)
    inp['a_log'] = jnp.log(jax.random.uniform(next(ks), (DEPTH, B_HEADS), F32, 1.0, 16.0))
    dt = jnp.exp(jax.random.uniform(next(ks), (DEPTH, B_HEADS), F32, math.log(1e-3), math.log(1e-1)))
    inp['dt_bias'] = dt + jnp.log(-jnp.expm1(-dt))
    inp['gdn_norm'] = gain((DEPTH, B_HEAD_DIM))
    inp['w_branch_a'] = nrm((DEPTH, A_HEADS * A_HEAD_DIM, d), (A_HEADS * A_HEAD_DIM) ** -0.5)
    inp['w_branch_b'] = nrm((DEPTH, B_HEADS * B_HEAD_DIM, d), (B_HEADS * B_HEAD_DIM) ** -0.5)
    inp['w_out'] = nrm((DEPTH, d, d), d ** -0.5)
    inp['w_ffn_up'] = nrm((N_DENSE, d, 2 * D_FF), d ** -0.5)
    inp['w_ffn_down'] = nrm((N_DENSE, D_FF, d), D_FF ** -0.5)
    inp['w_router'] = nrm((N_MOE, d, N_EXPERTS), d ** -0.5)
    inp['w_exp_up'] = nrm((N_MOE, N_EXPERTS, d, 2 * D_FF_EXPERT), d ** -0.5)
    inp['w_exp_down'] = nrm((N_MOE, N_EXPERTS, D_FF_EXPERT, d), D_FF_EXPERT ** -0.5)
    inp['final_norm'] = gain((d,))
    return inp


def reference(x_prompt, x_sample, cache_k_g0, cache_v_g0, cache_k_g1, cache_v_g1, cache_k_g2, cache_v_g2,
              state_conv, state_delta, c_prompt, c_sample, w_mod, b_mod, norm_mix, norm_ffn, w_in, conv_w,
              a_log, dt_bias, gdn_norm, w_branch_a, w_branch_b, w_out, w_ffn_up, w_ffn_down, w_router,
              w_exp_up, w_exp_down, final_norm):
    xp, xs = x_prompt, x_sample
    new_p = [[] for _ in range(8)]
    new_s = [[] for _ in range(8)]
    for l in range(DEPTH):
        mix_w = (w_in[l], conv_w[l], a_log[l], dt_bias[l], gdn_norm[l], w_branch_a[l], w_branch_b[l], w_out[l])
        if l % 2 == 0:
            ffn = functools.partial(swiglu, w_up=w_ffn_up[l // 2], w_down=w_ffn_down[l // 2])
        else:
            ffn = functools.partial(moe, w_router=w_router[l // 2], w_up=w_exp_up[l // 2],
                                    w_down=w_exp_down[l // 2])
        xp, st_p = trunk_layer(xp, c_prompt, 0, None, None, None, w_mod[l], b_mod[l], norm_mix[l],
                               norm_ffn[l], mix_w, ffn)
        kv_bufs = (cache_k_g0[l], cache_v_g0[l], cache_k_g1[l], cache_v_g1[l], cache_k_g2[l], cache_v_g2[l])
        xs, st_s = trunk_layer(xs, c_sample, PAST_LEN, kv_bufs, state_conv[l], state_delta[l], w_mod[l],
                               b_mod[l], norm_mix[l], norm_ffn[l], mix_w, ffn)
        for i in range(8):
            new_p[i].append(st_p[i])
            new_s[i].append(st_s[i])
    y_prompt = rmsnorm(xp, final_norm)
    y_sample = rmsnorm(xs, final_norm)
    k_g0_p, v_g0_p, k_g1_p, v_g1_p, k_g2_p, v_g2_p, conv_p, delta_p = [jnp.stack(t) for t in new_p]
    k_g0_s, v_g0_s, k_g1_s, v_g1_s, k_g2_s, v_g2_s, conv_s, delta_s = [jnp.stack(t) for t in new_s]
    return (y_prompt, y_sample, k_g0_p, v_g0_p, k_g1_p, v_g1_p, k_g2_p, v_g2_p, conv_p, delta_p,
            k_g0_s, v_g0_s, k_g1_s, v_g1_s, k_g2_s, v_g2_s, conv_s, delta_s)
```

```python
import functools
import math

import numpy as np
import jax
import jax.numpy as jnp
from jax import lax
from jax.experimental import pallas as pl
from jax.experimental.pallas import tpu as pltpu

F32 = jnp.float32
BF16 = jnp.bfloat16

D_MODEL = 1024
ATTN_GROUPS = ((128, 1), (512, 4), (2048, 16))
N_GROUPS = 3
A_HEADS = 4
A_HEAD_DIM = 64
A_WIDTH = A_HEADS * A_HEAD_DIM
LANES = 128
SLABS = A_WIDTH // LANES
HEADS_PER_SLAB = LANES // A_HEAD_DIM
ROPE_THETA = 10000.0
BAND_BLOCK = 128
B_HEADS = 8
B_HEAD_DIM = 128
B_WIDTH = B_HEADS * B_HEAD_DIM
CONV_W = 4
CONV_CH = 3 * B_WIDTH
PROMPT_CHUNK = 64
N_EXPERTS = 8
NORM_EPS = 1e-6
A_QKV = N_GROUPS * 3 * A_WIDTH
NEG = -1e30

BA_PAD = 256
MAIN_COLS = CONV_CH + 3 * D_MODEL + BA_PAD
COL_Z = CONV_CH // D_MODEL
COL_GA = COL_Z + 1
COL_GB = COL_Z + 2
COL_BA = (CONV_CH + 3 * D_MODEL) // BA_PAD

VMEM_LIMIT = 52 * 1024 * 1024


def _cparams(*sem):
    return pltpu.CompilerParams(dimension_semantics=sem, vmem_limit_bytes=VMEM_LIMIT)


def _sigmoid(x):
    return 1.0 / (1.0 + jnp.exp(-x))


def _silu(x):
    return x * _sigmoid(x)


def _nt_dot(a, b):
    return lax.dot_general(a, b, (((1,), (1,)), ((), ())), preferred_element_type=F32)


def _tn_dot(a, b):
    return lax.dot_general(a, b, (((0,), (0,)), ((), ())), preferred_element_type=F32)


def _mod_spec(per_seq, tm, tiles_per_seq):
    if per_seq:
        return pl.BlockSpec((None, 1, D_MODEL), lambda i, *_: (i // tiles_per_seq, 0, 0))
    return pl.BlockSpec((tm, D_MODEL), lambda i, *_: (i, 0))


def _rms_mod(x, nw, sc, sh):
    y = x * lax.rsqrt(jnp.mean(x * x, axis=-1, keepdims=True) + NORM_EPS) * nw
    return y * (1.0 + sc) + sh


def _mod_body(c_ref, w_ref, b_ref, o_ref):
    s = _silu(c_ref[...]).astype(BF16)
    o_ref[...] = jnp.dot(s, w_ref[...].astype(BF16), preferred_element_type=F32) + b_ref[...]


def _modulation(c_all, w_mod, b_mod):
    depth, d, n = w_mod.shape
    rows = c_all.shape[0]
    tn = 1536
    return pl.pallas_call(
        _mod_body,
        grid=(depth, n // tn),
        in_specs=[pl.BlockSpec((rows, d), lambda l, j: (0, 0)),
                  pl.BlockSpec((None, d, tn), lambda l, j: (l, 0, j)),
                  pl.BlockSpec((None, 1, tn), lambda l, j: (l, 0, j))],
        out_specs=pl.BlockSpec((None, rows, tn), lambda l, j: (l, 0, j)),
        out_shape=jax.ShapeDtypeStruct((depth, rows, n), F32),
        compiler_params=_cparams("arbitrary", "arbitrary"),
        name="adaln_mod",
    )(c_all, w_mod, b_mod.reshape(depth, 1, n))


def _proj_body(x_ref, sc_ref, sh_ref, nw_ref, w_ref, *rest, rope):
    if rope:
        cos_ref, sin_ref, o_ref, h_scr = rest
    else:
        o_ref, h_scr = rest

    @pl.when(pl.program_id(1) == 0)
    def _():
        h_scr[...] = _rms_mod(x_ref[...], nw_ref[...], sc_ref[...], sh_ref[...]).astype(BF16)

    y = jnp.dot(h_scr[...], w_ref[...], preferred_element_type=F32)
    if not rope:
        o_ref[...] = y
        return
    cos = cos_ref[...]
    sin = sin_ref[...]
    lane = lax.broadcasted_iota(jnp.int32, cos.shape, 1)
    first_half = (lane % A_HEAD_DIM) < (A_HEAD_DIM // 2)
    for c in range(3 * SLABS):
        t = y[:, c * LANES:(c + 1) * LANES]
        if c < 2 * SLABS:
            partner = jnp.where(first_half, pltpu.roll(t, LANES - A_HEAD_DIM // 2, 1),
                                pltpu.roll(t, A_HEAD_DIM // 2, 1))
            t = t * cos + partner * sin
        o_ref[c] = t


def _project(x, sc, sh, nw, w, *, tm, tn, per_seq, rope_tabs=None, name):
    rows, d = x.shape
    n = w.shape[1]
    tps = (rows // sc.shape[0]) // tm if per_seq else None
    in_specs = [pl.BlockSpec((tm, d), lambda i, j: (i, 0)),
                _mod_spec(per_seq, tm, tps), _mod_spec(per_seq, tm, tps),
                pl.BlockSpec((1, d), lambda i, j: (0, 0)),
                pl.BlockSpec((d, tn), lambda i, j: (0, j))]
    args = [x, sc, sh, nw, w]
    out_spec = pl.BlockSpec((tm, tn), lambda i, j: (i, j))
    out_shape = jax.ShapeDtypeStruct((rows, n), F32)
    if rope_tabs is not None:
        in_specs += [pl.BlockSpec((tm, LANES), lambda i, j: (i, 0))] * 2
        args += list(rope_tabs)
        out_spec = pl.BlockSpec((tn // LANES, tm, LANES), lambda i, j: (j, i, 0))
        out_shape = jax.ShapeDtypeStruct((n // LANES, rows, LANES), F32)
    return pl.pallas_call(
        functools.partial(_proj_body, rope=rope_tabs is not None),
        grid=(rows // tm, n // tn),
        in_specs=in_specs,
        out_specs=out_spec,
        out_shape=out_shape,
        scratch_shapes=[pltpu.VMEM((tm, d), BF16)],
        compiler_params=_cparams("arbitrary", "arbitrary"),
        name=name,
    )(*args)


def _head_masks(width):
    lane = lax.broadcasted_iota(jnp.int32, (1, width), 1)
    return [(lane // A_HEAD_DIM == h).astype(F32) for h in range(A_HEADS)]


def _merge_lse(o_a, l_a, o_b, l_b):
    m = jnp.maximum(l_a, l_b)
    e_a = jnp.exp(l_a - m)
    e_b = jnp.exp(l_b - m)
    den = e_a + e_b
    return (o_a * e_a + o_b * e_b) / den, m + jnp.log(den)


def _band_body(q_ref, kc_ref, kp_ref, vc_ref, vp_ref, bias_ref, *rest, dil, has_prev_group):
    if has_prev_group:
        oprev_ref, lprev_ref, o_ref, l_ref = rest
    else:
        o_ref, l_ref = rest
    blk = BAND_BLOCK
    col = lax.broadcasted_iota(jnp.int32, (blk, 2 * blk), 1)
    no_prev = jnp.logical_and(pl.program_id(1) == 0, col < blk)
    bias = jnp.where(no_prev, NEG, bias_ref[...])
    bias_h = jnp.concatenate([bias] * HEADS_PER_SLAB, axis=0)
    hm = _head_masks(LANES)

    def residue(r):
        rows = pl.ds(r, blk, stride=dil) if dil > 1 else pl.ds(0, blk)
        for sl in range(SLABS):
            q = q_ref[sl, rows, :] * (A_HEAD_DIM ** -0.5)
            k = jnp.concatenate([kp_ref[sl, rows, :], kc_ref[sl, rows, :]], axis=0).astype(BF16)
            v = jnp.concatenate([vp_ref[sl, rows, :], vc_ref[sl, rows, :]], axis=0).astype(BF16)
            qh = jnp.concatenate([q * hm[h] for h in range(HEADS_PER_SLAB)], axis=0).astype(BF16)
            s = _nt_dot(qh, k) + bias_h
            m = jnp.max(s, axis=1, keepdims=True)
            p = jnp.exp(s - m)
            l = jnp.sum(p, axis=1, keepdims=True)
            oh = jnp.dot(p.astype(BF16), v, preferred_element_type=F32) / l
            lseh = m + jnp.log(l)
            o = sum(oh[h * blk:(h + 1) * blk] * hm[h] for h in range(HEADS_PER_SLAB))
            lse = sum(lseh[h * blk:(h + 1) * blk] * hm[h] for h in range(HEADS_PER_SLAB))
            if has_prev_group:
                o, lse = _merge_lse(oprev_ref[sl, rows, :], lprev_ref[sl, rows, :], o, lse)
            o_ref[sl, rows, :] = o
            l_ref[sl, rows, :] = lse

    if dil <= 4:
        for r in range(dil):
            residue(r)
    else:
        def loop_body(r, carry):
            residue(r)
            return carry
        lax.fori_loop(0, dil, loop_body, 0)


def _band_bias(nk):
    qi = np.arange(BAND_BLOCK)[:, None]
    kj = np.arange(2 * BAND_BLOCK)[None, :]
    rel = kj - BAND_BLOCK - qi
    return jnp.asarray(np.where((rel <= 0) & (rel >= -nk), 0.0, NEG), F32)


def _band_attention(qkv, gi, seq_len, prev):
    win, dil = ATTN_GROUPS[gi]
    rows = qkv.shape[1]
    sb_rows = BAND_BLOCK * dil
    nsb = seq_len // sb_rows
    nseq = rows // seq_len

    def cur(c):
        return pl.BlockSpec((SLABS, sb_rows, LANES), lambda n, s: (c, n * nsb + s, 0))

    def prv(c):
        return pl.BlockSpec((SLABS, sb_rows, LANES), lambda n, s: (c, n * nsb + jnp.maximum(s - 1, 0), 0))

    row_spec = pl.BlockSpec((SLABS, sb_rows, LANES), lambda n, s: (0, n * nsb + s, 0))
    in_specs = [cur(3 * gi), cur(3 * gi + 1), prv(3 * gi + 1), cur(3 * gi + 2), prv(3 * gi + 2),
                pl.BlockSpec((BAND_BLOCK, 2 * BAND_BLOCK), lambda n, s: (0, 0))]
    args = [qkv, qkv, qkv, qkv, qkv, _band_bias(win // dil)]
    if prev is not None:
        in_specs += [row_spec, row_spec]
        args += list(prev)
    return pl.pallas_call(
        functools.partial(_band_body, dil=dil, has_prev_group=prev is not None),
        grid=(nseq, nsb),
        in_specs=in_specs,
        out_specs=[row_spec, row_spec],
        out_shape=[jax.ShapeDtypeStruct((SLABS, rows, LANES), F32)] * 2,
        compiler_params=_cparams("arbitrary", "arbitrary"),
        name="band_attn_g%d" % gi,
    )(*args)


def _cached_body(row_ref, *rest, n_new):
    caches = rest[:2 * N_GROUPS]
    biases = rest[2 * N_GROUPS:4 * N_GROUPS]
    o_ref = rest[4 * N_GROUPS]
    hm = _head_masks(A_WIDTH)
    o_run = l_run = None
    def slab_cols(c):
        return jnp.concatenate([row_ref[c * SLABS + sl] for sl in range(SLABS)], axis=1)

    for gi in range(N_GROUPS):
        q = slab_cols(3 * gi) * (A_HEAD_DIM ** -0.5)
        kn = slab_cols(3 * gi + 1).astype(BF16)
        vn = slab_cols(3 * gi + 2).astype(BF16)
        kc = caches[2 * gi][...].astype(BF16)
        vc = caches[2 * gi + 1][...].astype(BF16)
        q4 = jnp.concatenate([q * hm[h] for h in range(A_HEADS)], axis=0).astype(BF16)
        s_c = _nt_dot(q4, kc) + biases[2 * gi][...]
        s_n = _nt_dot(q4, kn) + biases[2 * gi + 1][...]
        m = jnp.maximum(jnp.max(s_c, axis=1, keepdims=True), jnp.max(s_n, axis=1, keepdims=True))
        p_c = jnp.exp(s_c - m)
        p_n = jnp.exp(s_n - m)
        l = jnp.sum(p_c, axis=1, keepdims=True) + jnp.sum(p_n, axis=1, keepdims=True)
        o4 = (jnp.dot(p_c.astype(BF16), vc, preferred_element_type=F32)
              + jnp.dot(p_n.astype(BF16), vn, preferred_element_type=F32)) / l
        lse4 = m + jnp.log(l)
        o = sum(o4[h * n_new:(h + 1) * n_new] * hm[h] for h in range(A_HEADS))
        lse = sum(lse4[h * n_new:(h + 1) * n_new] * hm[h] for h in range(A_HEADS))
        if gi == 0:
            o_run, l_run = o, lse
        else:
            o_run, l_run = _merge_lse(o_run, l_run, o, lse)
    for sl in range(SLABS):
        o_ref[sl] = o_run[:, sl * LANES:(sl + 1) * LANES]


def _cached_bias(wb, n_new, win, dil):
    nk = win // dil
    t = np.tile(np.arange(n_new), A_HEADS)[:, None]
    i = np.arange(wb + n_new)[None, :]
    delta = wb + t - i
    ok = (delta >= 0) & (delta % dil == 0) & (delta <= dil * nk)
    b = np.where(ok, 0.0, NEG).astype(np.float32)
    return jnp.asarray(b[:, :wb]), jnp.asarray(b[:, wb:])


def _cached_attention(qkv, caches, layer, n_new):
    nslab, rows, _ = qkv.shape
    nseq = rows // n_new
    in_specs = [pl.BlockSpec((nslab, n_new, LANES), lambda n: (0, n, 0))]
    args = [qkv]
    for c in caches:
        in_specs.append(pl.BlockSpec((None, None, c.shape[2], A_WIDTH), lambda n: (layer, n, 0, 0)))
        args.append(c)
    for gi, (win, dil) in enumerate(ATTN_GROUPS):
        wb = caches[2 * gi].shape[2]
        for b in _cached_bias(wb, n_new, win, dil):
            in_specs.append(pl.BlockSpec(b.shape, lambda n: (0, 0)))
            args.append(b)
    return pl.pallas_call(
        functools.partial(_cached_body, n_new=n_new),
        grid=(nseq,),
        in_specs=in_specs,
        out_specs=pl.BlockSpec((SLABS, n_new, LANES), lambda n: (0, n, 0)),
        out_shape=jax.ShapeDtypeStruct((SLABS, rows, LANES), F32),
        compiler_params=_cparams("arbitrary"),
        name="cached_attn",
    )(*args)


def _cumsum_rows(x):
    n = x.shape[0]
    row = lax.broadcasted_iota(jnp.int32, x.shape, 0)
    sh = 1
    while sh < n:
        x = x + jnp.where(row >= sh, pltpu.roll(x, sh, 0), 0.0)
        sh *= 2
    return x


def _delta_body(u_ref, ba_ref, conv0_ref, s0_ref, cw_ref, ap_ref, o_ref, s_ref, ubuf, *, chunk):
    C = chunk

    @pl.when(pl.program_id(1) == 0)
    def _():
        ubuf[0:8, :] = jnp.zeros((8, CONV_CH), F32)
        ubuf[8 - (CONV_W - 1):8, :] = conv0_ref[...]
        s_ref[...] = s0_ref[...]

    ubuf[8:8 + C, :] = u_ref[...]
    cw = cw_ref[...]
    conv = ubuf[8:8 + C, :] * cw[CONV_W - 1:CONV_W]
    for i in range(CONV_W - 1):
        off = 8 - (CONV_W - 1) + i
        conv = conv + ubuf[off:off + C, :] * cw[i:i + 1]
    ubuf[0:8, :] = ubuf[C:C + 8, :]
    cu = _silu(conv)

    ba = ba_ref[...]
    ap = ap_ref[...]
    beta_f = _sigmoid(ba)
    z = ba + ap[1:2]
    softplus = jnp.maximum(z, 0.0) + jnp.log(1.0 + jnp.exp(-jnp.abs(z)))
    g_cum = _cumsum_rows(-jnp.exp(ap[0:1]) * softplus)

    ri = lax.broadcasted_iota(jnp.int32, (C, C), 0)
    ci = lax.broadcasted_iota(jnp.int32, (C, C), 1)
    eye = ri == ci
    eye_f = eye.astype(F32)

    def to_row(colv):
        return jnp.sum(jnp.where(eye, colv, 0.0), axis=0, keepdims=True)

    for h in range(B_HEADS):
        lo = h * B_HEAD_DIM
        qh = cu[:, lo:lo + B_HEAD_DIM]
        kh = cu[:, B_WIDTH + lo:B_WIDTH + lo + B_HEAD_DIM]
        vh = cu[:, 2 * B_WIDTH + lo:2 * B_WIDTH + lo + B_HEAD_DIM]
        qn = qh * lax.rsqrt(jnp.sum(qh * qh, axis=-1, keepdims=True) + NORM_EPS) * (B_HEAD_DIM ** -0.5)
        kn = kh * lax.rsqrt(jnp.sum(kh * kh, axis=-1, keepdims=True) + NORM_EPS)
        beta = beta_f[:, h:h + 1]
        g_col = g_cum[:, B_HEADS + h:B_HEADS + h + 1]
        g_row = to_row(g_col)
        beta_row = to_row(beta)
        kb = kn.astype(BF16)
        kk = _nt_dot(kb, kb)
        dm = jnp.where(ri >= ci, jnp.exp(jnp.where(ri >= ci, g_col - g_row, 0.0)), 0.0)
        qk = _nt_dot(qn.astype(BF16), kb) * dm
        a_t = jnp.where(ci > ri, beta_row * kk * jnp.exp(jnp.where(ci > ri, g_row - g_col, 0.0)), 0.0)
        t_inv = eye_f
        for i in range(1, C):
            s = jnp.sum(a_t[:, i:i + 1] * t_inv, axis=0, keepdims=True)
            t_inv = t_inv - jnp.where(ri == i, s, 0.0)
        tb = t_inv.astype(BF16)
        e_g = jnp.exp(g_col)
        u = jnp.dot(tb, (beta * vh).astype(BF16), preferred_element_type=F32)
        w = jnp.dot(tb, (beta * e_g * kn).astype(BF16), preferred_element_type=F32)
        s_old = s_ref[h]
        sb = s_old.astype(BF16)
        wv = u - jnp.dot(w.astype(BF16), sb, preferred_element_type=F32)
        wvb = wv.astype(BF16)
        o = (jnp.dot((qn * e_g).astype(BF16), sb, preferred_element_type=F32)
             + jnp.dot(qk.astype(BF16), wvb, preferred_element_type=F32))
        g_last = g_col[C - 1:C, :]
        kd = (kn * jnp.exp(g_last - g_col)).astype(BF16)
        s_ref[h] = jnp.exp(g_last) * s_old + _tn_dot(kd, wvb)
        o_ref[:, lo:lo + B_HEAD_DIM] = o


def _delta_rule(proj_main, conv0, s0, layer, conv_w, a_par, seq_len, chunk):
    rows = proj_main.shape[0]
    nseq = rows // seq_len
    nc = seq_len // chunk
    state_spec = pl.BlockSpec((None, None, B_HEADS, B_HEAD_DIM, B_HEAD_DIM), lambda n, c: (layer, n, 0, 0, 0))
    return pl.pallas_call(
        functools.partial(_delta_body, chunk=chunk),
        grid=(nseq, nc),
        in_specs=[pl.BlockSpec((chunk, CONV_CH), lambda n, c: (n * nc + c, 0)),
                  pl.BlockSpec((chunk, BA_PAD), lambda n, c: (n * nc + c, COL_BA)),
                  pl.BlockSpec((None, None, CONV_W - 1, CONV_CH), lambda n, c: (layer, n, 0, 0)),
                  state_spec,
                  pl.BlockSpec((CONV_W, CONV_CH), lambda n, c: (0, 0)),
                  pl.BlockSpec((2, BA_PAD), lambda n, c: (0, 0))],
        out_specs=[pl.BlockSpec((chunk, B_WIDTH), lambda n, c: (n * nc + c, 0)),
                   pl.BlockSpec((None, B_HEADS, B_HEAD_DIM, B_HEAD_DIM), lambda n, c: (n, 0, 0, 0))],
        out_shape=[jax.ShapeDtypeStruct((rows, B_WIDTH), F32),
                   jax.ShapeDtypeStruct((nseq, B_HEADS, B_HEAD_DIM, B_HEAD_DIM), F32)],
        scratch_shapes=[pltpu.VMEM((chunk + 8, CONV_CH), F32)],
        compiler_params=_cparams("arbitrary", "arbitrary"),
        name="delta_c%d" % chunk,
    )(proj_main, proj_main, conv0, s0, conv_w, a_par)


def _mix_out_body(x_ref, oa_ref, ob_ref, z_ref, ga_ref, gb_ref, g1_ref, sc2_ref, sh2_ref,
                  gn_ref, nf_ref, wa_ref, wb_ref, wo_ref, xo_ref, h2_ref):
    oa = jnp.concatenate([oa_ref[sl] for sl in range(SLABS)], axis=1)
    ya = jnp.dot(oa.astype(BF16), wa_ref[...], preferred_element_type=F32)
    gn = gn_ref[...]
    parts = []
    for h in range(B_HEADS):
        lo = h * B_HEAD_DIM
        t = ob_ref[:, lo:lo + B_HEAD_DIM]
        t = t * lax.rsqrt(jnp.mean(t * t, axis=-1, keepdims=True) + NORM_EPS) * gn
        parts.append((t * _silu(z_ref[:, lo:lo + B_HEAD_DIM])).astype(BF16))
    yb = jnp.dot(jnp.concatenate(parts, axis=1), wb_ref[...], preferred_element_type=F32)
    merged = _sigmoid(ga_ref[...]) * ya + _sigmoid(gb_ref[...]) * yb
    mix = jnp.dot(merged.astype(BF16), wo_ref[...], preferred_element_type=F32)
    xn = x_ref[...] + g1_ref[...] * mix
    xo_ref[...] = xn
    h2_ref[...] = _rms_mod(xn, nf_ref[...], sc2_ref[...], sh2_ref[...]).astype(BF16)


def _mixer_out(x, o_a, o_b, proj_main, g1, sc2, sh2, gdn_norm, norm_ffn, wa, wb, wo, *, tm, per_seq):
    rows, d = x.shape
    tps = (rows // g1.shape[0]) // tm if per_seq else None
    ms = _mod_spec(per_seq, tm, tps)

    def const(a):
        return pl.BlockSpec(a.shape, lambda i: (0,) * a.ndim)

    def colblk(c):
        return pl.BlockSpec((tm, d), lambda i: (i, c))

    return pl.pallas_call(
        _mix_out_body,
        grid=(rows // tm,),
        in_specs=[colblk(0), pl.BlockSpec((SLABS, tm, LANES), lambda i: (0, i, 0)), colblk(0),
                  colblk(COL_Z), colblk(COL_GA), colblk(COL_GB), ms, ms, ms,
                  const(gdn_norm), const(norm_ffn), const(wa), const(wb), const(wo)],
        out_specs=[colblk(0), colblk(0)],
        out_shape=[jax.ShapeDtypeStruct((rows, d), F32), jax.ShapeDtypeStruct((rows, d), BF16)],
        compiler_params=_cparams("arbitrary"),
        name="mixer_out",
    )(x, o_a, o_b, proj_main, proj_main, proj_main, g1, sc2, sh2, gdn_norm, norm_ffn, wa, wb, wo)


def _ffn_body(h_ref, base_ref, g2_ref, wg_ref, wu_ref, wd_ref, *rest, expert):
    if expert is None:
        o_ref, acc = rest
    else:
        comb_ref, o_ref, acc = rest
    j = pl.program_id(1)

    @pl.when(j == 0)
    def _():
        acc[...] = jnp.zeros_like(acc)

    h = h_ref[...]
    gte = jnp.dot(h, wg_ref[...], preferred_element_type=F32)
    up = jnp.dot(h, wu_ref[...], preferred_element_type=F32)
    acc[...] += jnp.dot((_silu(gte) * up).astype(BF16), wd_ref[...], preferred_element_type=F32)

    @pl.when(j == pl.num_programs(1) - 1)
    def _():
        y = acc[...]
        if expert is not None:
            y = y * comb_ref[:, expert:expert + 1]
        o_ref[...] = base_ref[...] + g2_ref[...] * y


def _ffn(h2, base, g2, w_up, w_down, *, tm, tf, per_seq, lead=(), comb=None, expert=None):
    rows, d = h2.shape
    ff = w_down.shape[-2]
    nj = ff // tf
    nl = (None,) * len(lead)
    tps = (rows // g2.shape[0]) // tm if per_seq else None
    in_specs = [pl.BlockSpec((tm, d), lambda i, j: (i, 0)),
                pl.BlockSpec((tm, d), lambda i, j: (i, 0)),
                _mod_spec(per_seq, tm, tps),
                pl.BlockSpec(nl + (d, tf), lambda i, j: lead + (0, j)),
                pl.BlockSpec(nl + (d, tf), lambda i, j: lead + (0, j + nj)),
                pl.BlockSpec(nl + (tf, d), lambda i, j: lead + (j, 0))]
    args = [h2, base, g2, w_up, w_up, w_down]
    if comb is not None:
        in_specs.append(pl.BlockSpec((tm, comb.shape[1]), lambda i, j: (i, 0)))
        args.append(comb)
    return pl.pallas_call(
        functools.partial(_ffn_body, expert=expert),
        grid=(rows // tm, nj),
        in_specs=in_specs,
        out_specs=pl.BlockSpec((tm, d), lambda i, j: (i, 0)),
        out_shape=jax.ShapeDtypeStruct((rows, d), F32),
        scratch_shapes=[pltpu.VMEM((tm, d), F32)],
        compiler_params=_cparams("arbitrary", "arbitrary"),
        name="ffn" if expert is None else "ffn_e%d" % expert,
    )(*args)


def _router_body(x_ref, sc_ref, sh_ref, nf_ref, wr_ref, comb_ref):
    h = _rms_mod(x_ref[...], nf_ref[...], sc_ref[...], sh_ref[...])
    logits = jnp.dot(h, wr_ref[...], preferred_element_type=F32, precision=lax.Precision.HIGHEST)
    lane = lax.broadcasted_iota(jnp.int32, logits.shape, 1).astype(F32)
    width = float(logits.shape[1])
    logits = jnp.where(lane < N_EXPERTS, logits, NEG)
    m1 = jnp.max(logits, axis=1, keepdims=True)
    i1 = jnp.min(jnp.where(logits == m1, lane, width), axis=1, keepdims=True)
    rest = jnp.where(lane == i1, NEG, logits)
    m2 = jnp.max(rest, axis=1, keepdims=True)
    i2 = jnp.min(jnp.where(rest == m2, lane, width), axis=1, keepdims=True)
    e2 = jnp.exp(m2 - m1)
    w1 = 1.0 / (1.0 + e2)
    comb_ref[...] = jnp.where(lane == i1, w1, 0.0) + jnp.where(lane == i2, e2 * w1, 0.0)


def _router(x, sc2, sh2, norm_ffn, w_router_pad, *, tm, per_seq):
    rows, d = x.shape
    tps = (rows // sc2.shape[0]) // tm if per_seq else None
    ms = _mod_spec(per_seq, tm, tps)
    width = w_router_pad.shape[1]
    return pl.pallas_call(
        _router_body,
        grid=(rows // tm,),
        in_specs=[pl.BlockSpec((tm, d), lambda i: (i, 0)), ms, ms,
                  pl.BlockSpec((1, d), lambda i: (0, 0)),
                  pl.BlockSpec((d, width), lambda i: (0, 0))],
        out_specs=pl.BlockSpec((tm, width), lambda i: (i, 0)),
        out_shape=jax.ShapeDtypeStruct((rows, width), F32),
        compiler_params=_cparams("arbitrary"),
        name="router",
    )(x, sc2, sh2, norm_ffn, w_router_pad)


def _final_body(x_ref, w_ref, o_ref):
    x = x_ref[...]
    o_ref[...] = x * lax.rsqrt(jnp.mean(x * x, axis=-1, keepdims=True) + NORM_EPS) * w_ref[...]


def _final_norm(x, w, *, tm):
    rows, d = x.shape
    return pl.pallas_call(
        _final_body,
        grid=(rows // tm,),
        in_specs=[pl.BlockSpec((tm, d), lambda i: (i, 0)), pl.BlockSpec((1, d), lambda i: (0, 0))],
        out_specs=pl.BlockSpec((tm, d), lambda i: (i, 0)),
        out_shape=jax.ShapeDtypeStruct((rows, d), F32),
        compiler_params=_cparams("arbitrary"),
        name="final_norm",
    )(x, w)


def _rope_tables(pos):
    half = A_HEAD_DIM // 2
    inv = jnp.exp(-math.log(ROPE_THETA) * jnp.arange(half, dtype=F32) / half)
    ang = pos.astype(F32)[:, None] * inv[None, :]
    cos, sin = jnp.cos(ang), jnp.sin(ang)
    cos_t = jnp.tile(jnp.concatenate([cos, cos], axis=1), (1, HEADS_PER_SLAB))
    sin_t = jnp.tile(jnp.concatenate([-sin, sin], axis=1), (1, HEADS_PER_SLAB))
    return cos_t, sin_t


def _slab_rows(pa, c, nseq, seq_len, keep):
    t = pa[c * SLABS:(c + 1) * SLABS].reshape(SLABS, nseq, seq_len, HEADS_PER_SLAB, A_HEAD_DIM)
    t = t[:, :, seq_len - keep:]
    return t.transpose(1, 2, 0, 3, 4).reshape(nseq, keep, A_HEADS, A_HEAD_DIM)


def _split_w_in(w):
    o_b = A_QKV
    o_z = o_b + CONV_CH
    o_ba = o_z + B_WIDTH
    o_g = o_ba + 2 * B_HEADS
    pad = jnp.zeros((w.shape[0], BA_PAD - 2 * B_HEADS), w.dtype)
    main = jnp.concatenate([w[:, o_b:o_ba], w[:, o_g:], w[:, o_ba:o_g], pad], axis=1)
    return w[:, :A_QKV].astype(BF16), main.astype(BF16)


def kernel(x_prompt, x_sample, cache_k_g0, cache_v_g0, cache_k_g1, cache_v_g1, cache_k_g2, cache_v_g2,
           state_conv, state_delta, c_prompt, c_sample, w_mod, b_mod, norm_mix, norm_ffn, w_in, conv_w,
           a_log, dt_bias, gdn_norm, w_branch_a, w_branch_b, w_out, w_ffn_up, w_ffn_down, w_router,
           w_exp_up, w_exp_down, final_norm):
    depth = w_in.shape[0]
    nb, seq, d = x_prompt.shape
    ns, dec, _ = x_sample.shape
    past = cache_k_g2.shape[2]

    c_all = jnp.concatenate([c_prompt, c_sample, jnp.zeros((-(nb + ns) % 8, d), F32)], axis=0)
    mod_all = _modulation(c_all, w_mod, b_mod)

    caches = [c.reshape(c.shape[0], c.shape[1], c.shape[2], A_WIDTH)
              for c in (cache_k_g0, cache_v_g0, cache_k_g1, cache_v_g1, cache_k_g2, cache_v_g2)]
    tabs_p = _rope_tables(jnp.tile(jnp.arange(seq), nb))
    tabs_s = _rope_tables(jnp.tile(past + jnp.arange(dec), ns))
    zero_conv = jnp.zeros((depth, nb, CONV_W - 1, CONV_CH), F32)
    zero_state = jnp.zeros((depth, nb, B_HEADS, B_HEAD_DIM, B_HEAD_DIM), F32)

    xp = x_prompt.reshape(nb * seq, d)
    xs = x_sample.reshape(ns * dec, d)
    new_p = [[] for _ in range(8)]
    new_s = [[] for _ in range(8)]
    tm_p, tm_s = 512, 512

    for l in range(depth):
        mods_p = [m.reshape(nb, 1, d) for m in jnp.split(mod_all[l, :nb], 6, axis=-1)]
        mods_s = [jnp.repeat(m, dec, axis=0) for m in jnp.split(mod_all[l, nb:nb + ns], 6, axis=-1)]
        w_attn, w_main = _split_w_in(w_in[l])
        nw_mix = norm_mix[l].reshape(1, d)
        nw_ffn = norm_ffn[l].reshape(1, d)
        a_par = jnp.zeros((2, BA_PAD), F32)
        a_par = a_par.at[0, B_HEADS:2 * B_HEADS].set(a_log[l]).at[1, B_HEADS:2 * B_HEADS].set(dt_bias[l])
        wa = w_branch_a[l].astype(BF16)
        wb = w_branch_b[l].astype(BF16)
        wo = w_out[l].astype(BF16)
        gn = gdn_norm[l].reshape(1, B_HEAD_DIM)

        outs = []
        for (x, mods, per_seq, tm, tabs, is_prompt) in ((xp, mods_p, True, tm_p, tabs_p, True),
                                                        (xs, mods_s, False, tm_s, tabs_s, False)):
            sh1, sc1, g1, sh2, sc2, g2 = mods
            pm = _project(x, sc1, sh1, nw_mix, w_main, tm=tm, tn=MAIN_COLS // 5, per_seq=per_seq,
                          name="proj_main")
            pa = _project(x, sc1, sh1, nw_mix, w_attn, tm=tm, tn=3 * A_WIDTH, per_seq=per_seq,
                          rope_tabs=tabs, name="proj_attn")
            if is_prompt:
                run = None
                for gi in range(N_GROUPS):
                    run = _band_attention(pa, gi, seq, run)
                o_a = run[0]
                o_b, s_new = _delta_rule(pm, zero_conv, zero_state, l, conv_w[l], a_par, seq, PROMPT_CHUNK)
            else:
                o_a = _cached_attention(pa, caches, l, dec)
                o_b, s_new = _delta_rule(pm, state_conv, state_delta, l, conv_w[l], a_par, dec, dec)
            x, h2 = _mixer_out(x, o_a, o_b, pm, g1, sc2, sh2, gn, nw_ffn, wa, wb, wo, tm=tm, per_seq=per_seq)
            if l % 2 == 0:
                x = _ffn(h2, x, g2, w_ffn_up[l // 2].astype(BF16), w_ffn_down[l // 2].astype(BF16),
                         tm=tm, tf=1408, per_seq=per_seq)
            else:
                wr = jnp.zeros((d, 128), F32).at[:, :N_EXPERTS].set(w_router[l // 2])
                comb = _router(x, sc2, sh2, nw_ffn, wr, tm=tm, per_seq=per_seq)
                wu = w_exp_up.astype(BF16)
                wd = w_exp_down.astype(BF16)
                for e in range(N_EXPERTS):
                    x = _ffn(h2, x, g2, wu, wd, tm=tm, tf=896, per_seq=per_seq, lead=(l // 2, e),
                             comb=comb, expert=e)
            outs.append((x, pa, pm, s_new))

        (xp, pa_p, pm_p, s_p), (xs, pa_s, pm_s, s_s) = outs
        for gi, (win, dil) in enumerate(ATTN_GROUPS):
            keep = min(win, seq)
            new_p[2 * gi].append(_slab_rows(pa_p, 3 * gi + 1, nb, seq, keep))
            new_p[2 * gi + 1].append(_slab_rows(pa_p, 3 * gi + 2, nb, seq, keep))
            new_s[2 * gi].append(_slab_rows(pa_s, 3 * gi + 1, ns, dec, dec))
            new_s[2 * gi + 1].append(_slab_rows(pa_s, 3 * gi + 2, ns, dec, dec))
        new_p[6].append(pm_p.reshape(nb, seq, MAIN_COLS)[:, seq - (CONV_W - 1):, :CONV_CH])
        new_p[7].append(s_p)
        conv_s = jnp.concatenate([state_conv[l], pm_s.reshape(ns, dec, MAIN_COLS)[:, :, :CONV_CH]], axis=1)
        new_s[6].append(conv_s[:, -(CONV_W - 1):])
        new_s[7].append(s_s)

    y_prompt = _final_norm(xp, final_norm.reshape(1, d), tm=tm_p).reshape(nb, seq, d)
    y_sample = _final_norm(xs, final_norm.reshape(1, d), tm=tm_s).reshape(ns, dec, d)
    return (y_prompt, y_sample, *[jnp.stack(t) for t in new_p], *[jnp.stack(t) for t in new_s])
```

```python
import functools
import math

import numpy as np
import jax
import jax.numpy as jnp
from jax import lax
from jax.experimental import pallas as pl
from jax.experimental.pallas import tpu as pltpu

F32 = jnp.float32
BF16 = jnp.bfloat16

D_MODEL = 1024
ATTN_GROUPS = ((128, 1), (512, 4), (2048, 16))
N_GROUPS = 3
A_HEADS = 4
A_HEAD_DIM = 64
A_WIDTH = A_HEADS * A_HEAD_DIM
LANES = 128
SLABS = A_WIDTH // LANES
HEADS_PER_SLAB = LANES // A_HEAD_DIM
ROPE_THETA = 10000.0
BAND_BLOCK = 128
B_HEADS = 8
B_HEAD_DIM = 128
B_WIDTH = B_HEADS * B_HEAD_DIM
CONV_W = 4
CONV_CH = 3 * B_WIDTH
PROMPT_CHUNK = 64
SAMPLE_SEQS_PER_STEP = 4
N_EXPERTS = 8
NORM_EPS = 1e-6
A_QKV = N_GROUPS * 3 * A_WIDTH
NEG = -1e30

BA_PAD = 256
MAIN_COLS = CONV_CH + 3 * D_MODEL + BA_PAD
COL_Z = CONV_CH // D_MODEL
COL_GA = COL_Z + 1
COL_GB = COL_Z + 2
COL_BA = (CONV_CH + 3 * D_MODEL) // BA_PAD

VMEM_LIMIT = 52 * 1024 * 1024


def _cparams(*sem):
    return pltpu.CompilerParams(dimension_semantics=sem, vmem_limit_bytes=VMEM_LIMIT)


def _sigmoid(x):
    return 1.0 / (1.0 + jnp.exp(-x))


def _silu(x):
    return x * _sigmoid(x)


def _nt_dot(a, b):
    return lax.dot_general(a, b, (((1,), (1,)), ((), ())), preferred_element_type=F32)


def _tn_dot(a, b):
    return lax.dot_general(a, b, (((0,), (0,)), ((), ())), preferred_element_type=F32)


def _mod_spec(per_seq, tm, tiles_per_seq):
    if per_seq:
        return pl.BlockSpec((None, 1, D_MODEL), lambda i, *_: (i // tiles_per_seq, 0, 0))
    return pl.BlockSpec((tm, D_MODEL), lambda i, *_: (i, 0))


def _rms_mod(x, nw, sc, sh):
    y = x * lax.rsqrt(jnp.mean(x * x, axis=-1, keepdims=True) + NORM_EPS) * nw
    return y * (1.0 + sc) + sh


def _mod_body(c_ref, w_ref, b_ref, o_ref):
    s = _silu(c_ref[...]).astype(BF16)
    o_ref[...] = jnp.dot(s, w_ref[...].astype(BF16), preferred_element_type=F32) + b_ref[...]


def _modulation(c_all, w_mod, b_mod):
    depth, d, n = w_mod.shape
    rows = c_all.shape[0]
    tn = 1536
    return pl.pallas_call(
        _mod_body,
        grid=(depth, n // tn),
        in_specs=[pl.BlockSpec((rows, d), lambda l, j: (0, 0)),
                  pl.BlockSpec((None, d, tn), lambda l, j: (l, 0, j)),
                  pl.BlockSpec((None, 1, tn), lambda l, j: (l, 0, j))],
        out_specs=pl.BlockSpec((None, rows, tn), lambda l, j: (l, 0, j)),
        out_shape=jax.ShapeDtypeStruct((depth, rows, n), F32),
        compiler_params=_cparams("arbitrary", "arbitrary"),
        name="adaln_mod",
    )(c_all, w_mod, b_mod.reshape(depth, 1, n))


def _proj_body(x_ref, sc_ref, sh_ref, nw_ref, w_ref, *rest, rope):
    if rope:
        cos_ref, sin_ref, o_ref, h_scr = rest
    else:
        o_ref, h_scr = rest

    @pl.when(pl.program_id(1) == 0)
    def _():
        h_scr[...] = _rms_mod(x_ref[...], nw_ref[...], sc_ref[...], sh_ref[...]).astype(BF16)

    y = jnp.dot(h_scr[...], w_ref[...], preferred_element_type=F32)
    if not rope:
        o_ref[...] = y
        return
    cos = cos_ref[...]
    sin = sin_ref[...]
    lane = lax.broadcasted_iota(jnp.int32, cos.shape, 1)
    first_half = (lane % A_HEAD_DIM) < (A_HEAD_DIM // 2)
    for c in range(3 * SLABS):
        t = y[:, c * LANES:(c + 1) * LANES]
        if c < 2 * SLABS:
            partner = jnp.where(first_half, pltpu.roll(t, LANES - A_HEAD_DIM // 2, 1),
                                pltpu.roll(t, A_HEAD_DIM // 2, 1))
            t = t * cos + partner * sin
        o_ref[c] = t


def _project(x, sc, sh, nw, w, *, tm, tn, per_seq, rope_tabs=None, name):
    rows, d = x.shape
    n = w.shape[1]
    tps = (rows // sc.shape[0]) // tm if per_seq else None
    in_specs = [pl.BlockSpec((tm, d), lambda i, j: (i, 0)),
                _mod_spec(per_seq, tm, tps), _mod_spec(per_seq, tm, tps),
                pl.BlockSpec((1, d), lambda i, j: (0, 0)),
                pl.BlockSpec((d, tn), lambda i, j: (0, j))]
    args = [x, sc, sh, nw, w]
    out_spec = pl.BlockSpec((tm, tn), lambda i, j: (i, j))
    out_shape = jax.ShapeDtypeStruct((rows, n), F32)
    if rope_tabs is not None:
        in_specs += [pl.BlockSpec((tm, LANES), lambda i, j: (i, 0))] * 2
        args += list(rope_tabs)
        out_spec = pl.BlockSpec((tn // LANES, tm, LANES), lambda i, j: (j, i, 0))
        out_shape = jax.ShapeDtypeStruct((n // LANES, rows, LANES), F32)
    return pl.pallas_call(
        functools.partial(_proj_body, rope=rope_tabs is not None),
        grid=(rows // tm, n // tn),
        in_specs=in_specs,
        out_specs=out_spec,
        out_shape=out_shape,
        scratch_shapes=[pltpu.VMEM((tm, d), BF16)],
        compiler_params=_cparams("arbitrary", "arbitrary"),
        name=name,
    )(*args)


def _head_masks(width):
    lane = lax.broadcasted_iota(jnp.int32, (1, width), 1)
    return [(lane // A_HEAD_DIM == h).astype(F32) for h in range(A_HEADS)]


def _merge_lse(o_a, l_a, o_b, l_b):
    m = jnp.maximum(l_a, l_b)
    e_a = jnp.exp(l_a - m)
    e_b = jnp.exp(l_b - m)
    den = e_a + e_b
    return (o_a * e_a + o_b * e_b) / den, m + jnp.log(den)


def _band_body(q_ref, kc_ref, kp_ref, vc_ref, vp_ref, bias_ref, *rest, dil, has_prev_group):
    if has_prev_group:
        oprev_ref, lprev_ref, o_ref, l_ref = rest
    else:
        o_ref, l_ref = rest
    blk = BAND_BLOCK
    col = lax.broadcasted_iota(jnp.int32, (blk, 2 * blk), 1)
    no_prev = jnp.logical_and(pl.program_id(1) == 0, col < blk)
    bias = jnp.where(no_prev, NEG, bias_ref[...])
    bias_h = jnp.concatenate([bias] * HEADS_PER_SLAB, axis=0)
    hm = _head_masks(LANES)

    def residue(r):
        rows = pl.ds(r, blk, stride=dil) if dil > 1 else pl.ds(0, blk)
        for sl in range(SLABS):
            q = q_ref[sl, rows, :] * (A_HEAD_DIM ** -0.5)
            k = jnp.concatenate([kp_ref[sl, rows, :], kc_ref[sl, rows, :]], axis=0).astype(BF16)
            v = jnp.concatenate([vp_ref[sl, rows, :], vc_ref[sl, rows, :]], axis=0).astype(BF16)
            qh = jnp.concatenate([q * hm[h] for h in range(HEADS_PER_SLAB)], axis=0).astype(BF16)
            s = _nt_dot(qh, k) + bias_h
            m = jnp.max(s, axis=1, keepdims=True)
            p = jnp.exp(s - m)
            l = jnp.sum(p, axis=1, keepdims=True)
            oh = jnp.dot(p.astype(BF16), v, preferred_element_type=F32) / l
            lseh = m + jnp.log(l)
            o = sum(oh[h * blk:(h + 1) * blk] * hm[h] for h in range(HEADS_PER_SLAB))
            lse = sum(lseh[h * blk:(h + 1) * blk] * hm[h] for h in range(HEADS_PER_SLAB))
            if has_prev_group:
                o, lse = _merge_lse(oprev_ref[sl, rows, :], lprev_ref[sl, rows, :], o, lse)
            o_ref[sl, rows, :] = o
            l_ref[sl, rows, :] = lse

    if dil <= 4:
        for r in range(dil):
            residue(r)
    else:
        def loop_body(r, carry):
            residue(r)
            return carry
        lax.fori_loop(0, dil, loop_body, 0)


def _band_bias(nk):
    qi = np.arange(BAND_BLOCK)[:, None]
    kj = np.arange(2 * BAND_BLOCK)[None, :]
    rel = kj - BAND_BLOCK - qi
    return jnp.asarray(np.where((rel <= 0) & (rel >= -nk), 0.0, NEG), F32)


def _band_attention(qkv, gi, seq_len, prev):
    win, dil = ATTN_GROUPS[gi]
    rows = qkv.shape[1]
    sb_rows = BAND_BLOCK * dil
    nsb = seq_len // sb_rows
    nseq = rows // seq_len

    def cur(c):
        return pl.BlockSpec((SLABS, sb_rows, LANES), lambda n, s: (c, n * nsb + s, 0))

    def prv(c):
        return pl.BlockSpec((SLABS, sb_rows, LANES), lambda n, s: (c, n * nsb + jnp.maximum(s - 1, 0), 0))

    row_spec = pl.BlockSpec((SLABS, sb_rows, LANES), lambda n, s: (0, n * nsb + s, 0))
    in_specs = [cur(3 * gi), cur(3 * gi + 1), prv(3 * gi + 1), cur(3 * gi + 2), prv(3 * gi + 2),
                pl.BlockSpec((BAND_BLOCK, 2 * BAND_BLOCK), lambda n, s: (0, 0))]
    args = [qkv, qkv, qkv, qkv, qkv, _band_bias(win // dil)]
    if prev is not None:
        in_specs += [row_spec, row_spec]
        args += list(prev)
    return pl.pallas_call(
        functools.partial(_band_body, dil=dil, has_prev_group=prev is not None),
        grid=(nseq, nsb),
        in_specs=in_specs,
        out_specs=[row_spec, row_spec],
        out_shape=[jax.ShapeDtypeStruct((SLABS, rows, LANES), F32)] * 2,
        compiler_params=_cparams("arbitrary", "arbitrary"),
        name="band_attn_g%d" % gi,
    )(*args)


def _cached_body(row_ref, *rest, n_new):
    caches = rest[:2 * N_GROUPS]
    biases = rest[2 * N_GROUPS:4 * N_GROUPS]
    o_ref = rest[4 * N_GROUPS]
    hm = _head_masks(A_WIDTH)
    o_run = l_run = None
    def slab_cols(c):
        return jnp.concatenate([row_ref[c * SLABS + sl] for sl in range(SLABS)], axis=1)

    for gi in range(N_GROUPS):
        q = slab_cols(3 * gi) * (A_HEAD_DIM ** -0.5)
        kn = slab_cols(3 * gi + 1).astype(BF16)
        vn = slab_cols(3 * gi + 2).astype(BF16)
        kc = caches[2 * gi][...].astype(BF16)
        vc = caches[2 * gi + 1][...].astype(BF16)
        q4 = jnp.concatenate([q * hm[h] for h in range(A_HEADS)], axis=0).astype(BF16)
        s_c = _nt_dot(q4, kc) + biases[2 * gi][...]
        s_n = _nt_dot(q4, kn) + biases[2 * gi + 1][...]
        m = jnp.maximum(jnp.max(s_c, axis=1, keepdims=True), jnp.max(s_n, axis=1, keepdims=True))
        p_c = jnp.exp(s_c - m)
        p_n = jnp.exp(s_n - m)
        l = jnp.sum(p_c, axis=1, keepdims=True) + jnp.sum(p_n, axis=1, keepdims=True)
        o4 = (jnp.dot(p_c.astype(BF16), vc, preferred_element_type=F32)
              + jnp.dot(p_n.astype(BF16), vn, preferred_element_type=F32)) / l
        lse4 = m + jnp.log(l)
        o = sum(o4[h * n_new:(h + 1) * n_new] * hm[h] for h in range(A_HEADS))
        lse = sum(lse4[h * n_new:(h + 1) * n_new] * hm[h] for h in range(A_HEADS))
        if gi == 0:
            o_run, l_run = o, lse
        else:
            o_run, l_run = _merge_lse(o_run, l_run, o, lse)
    for sl in range(SLABS):
        o_ref[sl] = o_run[:, sl * LANES:(sl + 1) * LANES]


def _cached_bias(wb, n_new, win, dil):
    nk = win // dil
    t = np.tile(np.arange(n_new), A_HEADS)[:, None]
    i = np.arange(wb + n_new)[None, :]
    delta = wb + t - i
    ok = (delta >= 0) & (delta % dil == 0) & (delta <= dil * nk)
    b = np.where(ok, 0.0, NEG).astype(np.float32)
    return jnp.asarray(b[:, :wb]), jnp.asarray(b[:, wb:])


def _cached_attention(qkv, caches, layer, n_new):
    nslab, rows, _ = qkv.shape
    nseq = rows // n_new
    in_specs = [pl.BlockSpec((nslab, n_new, LANES), lambda n: (0, n, 0))]
    args = [qkv]
    for c in caches:
        in_specs.append(pl.BlockSpec((None, None, c.shape[2], A_WIDTH), lambda n: (layer, n, 0, 0)))
        args.append(c)
    for gi, (win, dil) in enumerate(ATTN_GROUPS):
        wb = caches[2 * gi].shape[2]
        for b in _cached_bias(wb, n_new, win, dil):
            in_specs.append(pl.BlockSpec(b.shape, lambda n: (0, 0)))
            args.append(b)
    return pl.pallas_call(
        functools.partial(_cached_body, n_new=n_new),
        grid=(nseq,),
        in_specs=in_specs,
        out_specs=pl.BlockSpec((SLABS, n_new, LANES), lambda n: (0, n, 0)),
        out_shape=jax.ShapeDtypeStruct((SLABS, rows, LANES), F32),
        compiler_params=_cparams("arbitrary"),
        name="cached_attn",
    )(*args)


def _cumsum_rows(x):
    n = x.shape[0]
    row = lax.broadcasted_iota(jnp.int32, x.shape, 0)
    sh = 1
    while sh < n:
        x = x + jnp.where(row >= sh, pltpu.roll(x, sh, 0), 0.0)
        sh *= 2
    return x


def _delta_body(u_ref, ba_ref, conv0_ref, s0_ref, cw_ref, ap_ref, o_ref, s_ref, ubuf, *, chunk, nsb):
    C = chunk
    f32dot = functools.partial(jnp.dot, preferred_element_type=F32)

    @pl.when(pl.program_id(1) == 0)
    def _():
        ubuf[:, 0:8, :] = jnp.zeros((nsb, 8, CONV_CH), F32)
        ubuf[:, 8 - (CONV_W - 1):8, :] = conv0_ref[...]
        s_ref[...] = s0_ref[...]

    ubuf[:, 8:8 + C, :] = u_ref[...]
    cw = cw_ref[...]
    ap = ap_ref[...]
    ri = lax.broadcasted_iota(jnp.int32, (C, C), 0)
    ci = lax.broadcasted_iota(jnp.int32, (C, C), 1)
    eye_f = (ri == ci).astype(F32)
    levels = []
    b = 1
    while b < C:
        levels.append(((ri // (2 * b)) == (ci // (2 * b))) & ((ri // b) % 2 == 1) & ((ci // b) % 2 == 0))
        b *= 2

    units = []
    for sq in range(nsb):
        conv = ubuf[sq, 8:8 + C, :] * cw[CONV_W - 1:CONV_W]
        for i in range(CONV_W - 1):
            off = 8 - (CONV_W - 1) + i
            conv = conv + ubuf[sq, off:off + C, :] * cw[i:i + 1]
        cu = _silu(conv)
        ba = ba_ref[sq]
        beta_f = _sigmoid(ba)
        z = ba + ap[1:2]
        softplus = jnp.maximum(z, 0.0) + jnp.log(1.0 + jnp.exp(-jnp.abs(z)))
        g_cum = _cumsum_rows(-jnp.exp(ap[0:1]) * softplus)
        g_rows = g_cum[:, 0:LANES].T
        for h in range(B_HEADS):
            lo = h * B_HEAD_DIM
            qh = cu[:, lo:lo + B_HEAD_DIM]
            kh = cu[:, B_WIDTH + lo:B_WIDTH + lo + B_HEAD_DIM]
            qn = qh * lax.rsqrt(jnp.sum(qh * qh, axis=-1, keepdims=True) + NORM_EPS) * (B_HEAD_DIM ** -0.5)
            kn = kh * lax.rsqrt(jnp.sum(kh * kh, axis=-1, keepdims=True) + NORM_EPS)
            g_col = g_cum[:, B_HEADS + h:B_HEADS + h + 1]
            units.append(dict(sq=sq, h=h, qn=qn, kn=kn, kb=kn.astype(BF16),
                              vh=cu[:, 2 * B_WIDTH + lo:2 * B_WIDTH + lo + B_HEAD_DIM],
                              beta=beta_f[:, h:h + 1], g_col=g_col,
                              g_row=g_rows[B_HEADS + h:B_HEADS + h + 1, :]))
    ubuf[:, 0:8, :] = ubuf[:, C:C + 8, :]

    for un in units:
        un["qkk"] = _nt_dot(jnp.concatenate([un["kn"], un["qn"]], axis=0).astype(BF16), un["kb"])
    for un in units:
        dm = jnp.where(ri >= ci, jnp.exp(jnp.where(ri >= ci, un["g_col"] - un["g_row"], 0.0)), 0.0)
        a = jnp.where(ri > ci, un["beta"] * un["qkk"][0:C] * dm, 0.0)
        un["qk"] = (un["qkk"][C:2 * C] * dm).astype(BF16)
        un["ab"] = a.astype(BF16)
        un["t"] = eye_f - jnp.where(levels[0], a, 0.0)
    for lvl in levels[1:]:
        for un in units:
            un["tb"] = un["t"].astype(BF16)
            un["ta"] = f32dot(un["tb"], un["ab"]).astype(BF16)
        for un in units:
            un["t"] = un["t"] - jnp.where(lvl, f32dot(un["ta"], un["tb"]), 0.0)
    for un in units:
        e_g = jnp.exp(un["g_col"])
        rhs = jnp.concatenate([un["beta"] * un["vh"], un["beta"] * e_g * un["kn"]], axis=1).astype(BF16)
        un["uw"] = f32dot(un["t"].astype(BF16), rhs)
        un["qe"] = un["qn"] * e_g
    for un in units:
        s_old = s_ref[un["sq"], un["h"]]
        lhs = jnp.concatenate([un["uw"][:, B_HEAD_DIM:], un["qe"]], axis=0).astype(BF16)
        un["ws_qs"] = f32dot(lhs, s_old.astype(BF16))
    for un in units:
        sq, h = un["sq"], un["h"]
        wvb = (un["uw"][:, :B_HEAD_DIM] - un["ws_qs"][0:C]).astype(BF16)
        g_last = un["g_col"][C - 1:C, :]
        kd = (un["kn"] * jnp.exp(g_last - un["g_col"])).astype(BF16)
        o_ref[sq, :, h * B_HEAD_DIM:(h + 1) * B_HEAD_DIM] = un["ws_qs"][C:2 * C] + f32dot(un["qk"], wvb)
        s_ref[sq, h] = jnp.exp(g_last) * s_ref[sq, h] + _tn_dot(kd, wvb)


def _delta_rule(proj_main, conv0, s0, layer, conv_w, a_par, seq_len, chunk, nsb):
    rows, cols = proj_main.shape
    nseq = rows // seq_len
    nc = seq_len // chunk
    pm3 = proj_main.reshape(nseq, seq_len, cols)
    state_dims = (B_HEADS, B_HEAD_DIM, B_HEAD_DIM)
    o_b, s_new = pl.pallas_call(
        functools.partial(_delta_body, chunk=chunk, nsb=nsb),
        grid=(nseq // nsb, nc),
        in_specs=[pl.BlockSpec((nsb, chunk, CONV_CH), lambda n, c: (n, c, 0)),
                  pl.BlockSpec((nsb, chunk, BA_PAD), lambda n, c: (n, c, COL_BA)),
                  pl.BlockSpec((None, nsb, CONV_W - 1, CONV_CH), lambda n, c: (layer, n, 0, 0)),
                  pl.BlockSpec((None, nsb) + state_dims, lambda n, c: (layer, n, 0, 0, 0)),
                  pl.BlockSpec((CONV_W, CONV_CH), lambda n, c: (0, 0)),
                  pl.BlockSpec((2, BA_PAD), lambda n, c: (0, 0))],
        out_specs=[pl.BlockSpec((nsb, chunk, B_WIDTH), lambda n, c: (n, c, 0)),
                   pl.BlockSpec((nsb,) + state_dims, lambda n, c: (n, 0, 0, 0))],
        out_shape=[jax.ShapeDtypeStruct((nseq, seq_len, B_WIDTH), F32),
                   jax.ShapeDtypeStruct((nseq,) + state_dims, F32)],
        scratch_shapes=[pltpu.VMEM((nsb, chunk + 8, CONV_CH), F32)],
        compiler_params=_cparams("arbitrary", "arbitrary"),
        name="delta_c%d" % chunk,
    )(pm3, pm3, conv0, s0, conv_w, a_par)
    return o_b.reshape(rows, B_WIDTH), s_new


def _mix_out_body(x_ref, oa_ref, ob_ref, z_ref, ga_ref, gb_ref, g1_ref, sc2_ref, sh2_ref,
                  gn_ref, nf_ref, wa_ref, wb_ref, wo_ref, xo_ref, h2_ref):
    oa = jnp.concatenate([oa_ref[sl] for sl in range(SLABS)], axis=1)
    ya = jnp.dot(oa.astype(BF16), wa_ref[...], preferred_element_type=F32)
    gn = gn_ref[...]
    parts = []
    for h in range(B_HEADS):
        lo = h * B_HEAD_DIM
        t = ob_ref[:, lo:lo + B_HEAD_DIM]
        t = t * lax.rsqrt(jnp.mean(t * t, axis=-1, keepdims=True) + NORM_EPS) * gn
        parts.append((t * _silu(z_ref[:, lo:lo + B_HEAD_DIM])).astype(BF16))
    yb = jnp.dot(jnp.concatenate(parts, axis=1), wb_ref[...], preferred_element_type=F32)
    merged = _sigmoid(ga_ref[...]) * ya + _sigmoid(gb_ref[...]) * yb
    mix = jnp.dot(merged.astype(BF16), wo_ref[...], preferred_element_type=F32)
    xn = x_ref[...] + g1_ref[...] * mix
    xo_ref[...] = xn
    h2_ref[...] = _rms_mod(xn, nf_ref[...], sc2_ref[...], sh2_ref[...]).astype(BF16)


def _mixer_out(x, o_a, o_b, proj_main, g1, sc2, sh2, gdn_norm, norm_ffn, wa, wb, wo, *, tm, per_seq):
    rows, d = x.shape
    tps = (rows // g1.shape[0]) // tm if per_seq else None
    ms = _mod_spec(per_seq, tm, tps)

    def const(a):
        return pl.BlockSpec(a.shape, lambda i: (0,) * a.ndim)

    def colblk(c):
        return pl.BlockSpec((tm, d), lambda i: (i, c))

    return pl.pallas_call(
        _mix_out_body,
        grid=(rows // tm,),
        in_specs=[colblk(0), pl.BlockSpec((SLABS, tm, LANES), lambda i: (0, i, 0)), colblk(0),
                  colblk(COL_Z), colblk(COL_GA), colblk(COL_GB), ms, ms, ms,
                  const(gdn_norm), const(norm_ffn), const(wa), const(wb), const(wo)],
        out_specs=[colblk(0), colblk(0)],
        out_shape=[jax.ShapeDtypeStruct((rows, d), F32), jax.ShapeDtypeStruct((rows, d), BF16)],
        compiler_params=_cparams("arbitrary"),
        name="mixer_out",
    )(x, o_a, o_b, proj_main, proj_main, proj_main, g1, sc2, sh2, gdn_norm, norm_ffn, wa, wb, wo)


def _ffn_body(h_ref, base_ref, g2_ref, wg_ref, wu_ref, wd_ref, *rest, expert):
    if expert is None:
        o_ref, acc = rest
    else:
        comb_ref, o_ref, acc = rest
    j = pl.program_id(1)

    @pl.when(j == 0)
    def _():
        acc[...] = jnp.zeros_like(acc)

    h = h_ref[...]
    gte = jnp.dot(h, wg_ref[...], preferred_element_type=F32)
    up = jnp.dot(h, wu_ref[...], preferred_element_type=F32)
    acc[...] += jnp.dot((_silu(gte) * up).astype(BF16), wd_ref[...], preferred_element_type=F32)

    @pl.when(j == pl.num_programs(1) - 1)
    def _():
        y = acc[...]
        if expert is not None:
            y = y * comb_ref[:, expert:expert + 1]
        o_ref[...] = base_ref[...] + g2_ref[...] * y


def _ffn(h2, base, g2, w_up, w_down, *, tm, tf, per_seq, lead=(), comb=None, expert=None):
    rows, d = h2.shape
    ff = w_down.shape[-2]
    nj = ff // tf
    nl = (None,) * len(lead)
    tps = (rows // g2.shape[0]) // tm if per_seq else None
    in_specs = [pl.BlockSpec((tm, d), lambda i, j: (i, 0)),
                pl.BlockSpec((tm, d), lambda i, j: (i, 0)),
                _mod_spec(per_seq, tm, tps),
                pl.BlockSpec(nl + (d, tf), lambda i, j: lead + (0, j)),
                pl.BlockSpec(nl + (d, tf), lambda i, j: lead + (0, j + nj)),
                pl.BlockSpec(nl + (tf, d), lambda i, j: lead + (j, 0))]
    args = [h2, base, g2, w_up, w_up, w_down]
    if comb is not None:
        in_specs.append(pl.BlockSpec((tm, comb.shape[1]), lambda i, j: (i, 0)))
        args.append(comb)
    return pl.pallas_call(
        functools.partial(_ffn_body, expert=expert),
        grid=(rows // tm, nj),
        in_specs=in_specs,
        out_specs=pl.BlockSpec((tm, d), lambda i, j: (i, 0)),
        out_shape=jax.ShapeDtypeStruct((rows, d), F32),
        scratch_shapes=[pltpu.VMEM((tm, d), F32)],
        compiler_params=_cparams("arbitrary", "arbitrary"),
        name="ffn" if expert is None else "ffn_e%d" % expert,
    )(*args)


def _router_body(x_ref, sc_ref, sh_ref, nf_ref, wr_ref, comb_ref):
    h = _rms_mod(x_ref[...], nf_ref[...], sc_ref[...], sh_ref[...])
    logits = jnp.dot(h, wr_ref[...], preferred_element_type=F32, precision=lax.Precision.HIGHEST)
    lane = lax.broadcasted_iota(jnp.int32, logits.shape, 1).astype(F32)
    width = float(logits.shape[1])
    logits = jnp.where(lane < N_EXPERTS, logits, NEG)
    m1 = jnp.max(logits, axis=1, keepdims=True)
    i1 = jnp.min(jnp.where(logits == m1, lane, width), axis=1, keepdims=True)
    rest = jnp.where(lane == i1, NEG, logits)
    m2 = jnp.max(rest, axis=1, keepdims=True)
    i2 = jnp.min(jnp.where(rest == m2, lane, width), axis=1, keepdims=True)
    e2 = jnp.exp(m2 - m1)
    w1 = 1.0 / (1.0 + e2)
    comb_ref[...] = jnp.where(lane == i1, w1, 0.0) + jnp.where(lane == i2, e2 * w1, 0.0)


def _router(x, sc2, sh2, norm_ffn, w_router_pad, *, tm, per_seq):
    rows, d = x.shape
    tps = (rows // sc2.shape[0]) // tm if per_seq else None
    ms = _mod_spec(per_seq, tm, tps)
    width = w_router_pad.shape[1]
    return pl.pallas_call(
        _router_body,
        grid=(rows // tm,),
        in_specs=[pl.BlockSpec((tm, d), lambda i: (i, 0)), ms, ms,
                  pl.BlockSpec((1, d), lambda i: (0, 0)),
                  pl.BlockSpec((d, width), lambda i: (0, 0))],
        out_specs=pl.BlockSpec((tm, width), lambda i: (i, 0)),
        out_shape=jax.ShapeDtypeStruct((rows, width), F32),
        compiler_params=_cparams("arbitrary"),
        name="router",
    )(x, sc2, sh2, norm_ffn, w_router_pad)


def _final_body(x_ref, w_ref, o_ref):
    x = x_ref[...]
    o_ref[...] = x * lax.rsqrt(jnp.mean(x * x, axis=-1, keepdims=True) + NORM_EPS) * w_ref[...]


def _final_norm(x, w, *, tm):
    rows, d = x.shape
    return pl.pallas_call(
        _final_body,
        grid=(rows // tm,),
        in_specs=[pl.BlockSpec((tm, d), lambda i: (i, 0)), pl.BlockSpec((1, d), lambda i: (0, 0))],
        out_specs=pl.BlockSpec((tm, d), lambda i: (i, 0)),
        out_shape=jax.ShapeDtypeStruct((rows, d), F32),
        compiler_params=_cparams("arbitrary"),
        name="final_norm",
    )(x, w)


def _rope_tables(pos):
    half = A_HEAD_DIM // 2
    inv = jnp.exp(-math.log(ROPE_THETA) * jnp.arange(half, dtype=F32) / half)
    ang = pos.astype(F32)[:, None] * inv[None, :]
    cos, sin = jnp.cos(ang), jnp.sin(ang)
    cos_t = jnp.tile(jnp.concatenate([cos, cos], axis=1), (1, HEADS_PER_SLAB))
    sin_t = jnp.tile(jnp.concatenate([-sin, sin], axis=1), (1, HEADS_PER_SLAB))
    return cos_t, sin_t


def _slab_rows(pa, c, nseq, seq_len, keep):
    t = pa[c * SLABS:(c + 1) * SLABS].reshape(SLABS, nseq, seq_len, HEADS_PER_SLAB, A_HEAD_DIM)
    t = t[:, :, seq_len - keep:]
    return t.transpose(1, 2, 0, 3, 4).reshape(nseq, keep, A_HEADS, A_HEAD_DIM)


def _split_w_in(w):
    o_b = A_QKV
    o_z = o_b + CONV_CH
    o_ba = o_z + B_WIDTH
    o_g = o_ba + 2 * B_HEADS
    pad = jnp.zeros((w.shape[0], BA_PAD - 2 * B_HEADS), w.dtype)
    main = jnp.concatenate([w[:, o_b:o_ba], w[:, o_g:], w[:, o_ba:o_g], pad], axis=1)
    return w[:, :A_QKV].astype(BF16), main.astype(BF16)


def kernel(x_prompt, x_sample, cache_k_g0, cache_v_g0, cache_k_g1, cache_v_g1, cache_k_g2, cache_v_g2,
           state_conv, state_delta, c_prompt, c_sample, w_mod, b_mod, norm_mix, norm_ffn, w_in, conv_w,
           a_log, dt_bias, gdn_norm, w_branch_a, w_branch_b, w_out, w_ffn_up, w_ffn_down, w_router,
           w_exp_up, w_exp_down, final_norm):
    depth = w_in.shape[0]
    nb, seq, d = x_prompt.shape
    ns, dec, _ = x_sample.shape
    past = cache_k_g2.shape[2]

    c_all = jnp.concatenate([c_prompt, c_sample, jnp.zeros((-(nb + ns) % 8, d), F32)], axis=0)
    mod_all = _modulation(c_all, w_mod, b_mod)

    caches = [c.reshape(c.shape[0], c.shape[1], c.shape[2], A_WIDTH)
              for c in (cache_k_g0, cache_v_g0, cache_k_g1, cache_v_g1, cache_k_g2, cache_v_g2)]
    tabs_p = _rope_tables(jnp.tile(jnp.arange(seq), nb))
    tabs_s = _rope_tables(jnp.tile(past + jnp.arange(dec), ns))
    zero_conv = jnp.zeros((depth, nb, CONV_W - 1, CONV_CH), F32)
    zero_state = jnp.zeros((depth, nb, B_HEADS, B_HEAD_DIM, B_HEAD_DIM), F32)

    xp = x_prompt.reshape(nb * seq, d)
    xs = x_sample.reshape(ns * dec, d)
    new_p = [[] for _ in range(8)]
    new_s = [[] for _ in range(8)]
    tm_p, tm_s = 512, 512

    for l in range(depth):
        mods_p = [m.reshape(nb, 1, d) for m in jnp.split(mod_all[l, :nb], 6, axis=-1)]
        mods_s = [jnp.repeat(m, dec, axis=0) for m in jnp.split(mod_all[l, nb:nb + ns], 6, axis=-1)]
        w_attn, w_main = _split_w_in(w_in[l])
        nw_mix = norm_mix[l].reshape(1, d)
        nw_ffn = norm_ffn[l].reshape(1, d)
        a_par = jnp.zeros((2, BA_PAD), F32)
        a_par = a_par.at[0, B_HEADS:2 * B_HEADS].set(a_log[l]).at[1, B_HEADS:2 * B_HEADS].set(dt_bias[l])
        wa = w_branch_a[l].astype(BF16)
        wb = w_branch_b[l].astype(BF16)
        wo = w_out[l].astype(BF16)
        gn = gdn_norm[l].reshape(1, B_HEAD_DIM)

        outs = []
        for (x, mods, per_seq, tm, tabs, is_prompt) in ((xp, mods_p, True, tm_p, tabs_p, True),
                                                        (xs, mods_s, False, tm_s, tabs_s, False)):
            sh1, sc1, g1, sh2, sc2, g2 = mods
            pm = _project(x, sc1, sh1, nw_mix, w_main, tm=tm, tn=MAIN_COLS // 5, per_seq=per_seq,
                          name="proj_main")
            pa = _project(x, sc1, sh1, nw_mix, w_attn, tm=tm, tn=3 * A_WIDTH, per_seq=per_seq,
                          rope_tabs=tabs, name="proj_attn")
            if is_prompt:
                run = None
                for gi in range(N_GROUPS):
                    run = _band_attention(pa, gi, seq, run)
                o_a = run[0]
                o_b, s_new = _delta_rule(pm, zero_conv, zero_state, l, conv_w[l], a_par, seq, PROMPT_CHUNK, nb)
            else:
                o_a = _cached_attention(pa, caches, l, dec)
                o_b, s_new = _delta_rule(pm, state_conv, state_delta, l, conv_w[l], a_par, dec, dec, SAMPLE_SEQS_PER_STEP)
            x, h2 = _mixer_out(x, o_a, o_b, pm, g1, sc2, sh2, gn, nw_ffn, wa, wb, wo, tm=tm, per_seq=per_seq)
            if l % 2 == 0:
                x = _ffn(h2, x, g2, w_ffn_up[l // 2].astype(BF16), w_ffn_down[l // 2].astype(BF16),
                         tm=tm, tf=1408, per_seq=per_seq)
            else:
                wr = jnp.zeros((d, 128), F32).at[:, :N_EXPERTS].set(w_router[l // 2])
                comb = _router(x, sc2, sh2, nw_ffn, wr, tm=tm, per_seq=per_seq)
                wu = w_exp_up.astype(BF16)
                wd = w_exp_down.astype(BF16)
                for e in range(N_EXPERTS):
                    x = _ffn(h2, x, g2, wu, wd, tm=tm, tf=896, per_seq=per_seq, lead=(l // 2, e),
                             comb=comb, expert=e)
            outs.append((x, pa, pm, s_new))

        (xp, pa_p, pm_p, s_p), (xs, pa_s, pm_s, s_s) = outs
        for gi, (win, dil) in enumerate(ATTN_GROUPS):
            keep = min(win, seq)
            new_p[2 * gi].append(_slab_rows(pa_p, 3 * gi + 1, nb, seq, keep))
            new_p[2 * gi + 1].append(_slab_rows(pa_p, 3 * gi + 2, nb, seq, keep))
            new_s[2 * gi].append(_slab_rows(pa_s, 3 * gi + 1, ns, dec, dec))
            new_s[2 * gi + 1].append(_slab_rows(pa_s, 3 * gi + 2, ns, dec, dec))
        new_p[6].append(pm_p.reshape(nb, seq, MAIN_COLS)[:, seq - (CONV_W - 1):, :CONV_CH])
        new_p[7].append(s_p)
        conv_s = jnp.concatenate([state_conv[l], pm_s.reshape(ns, dec, MAIN_COLS)[:, :, :CONV_CH]], axis=1)
        new_s[6].append(conv_s[:, -(CONV_W - 1):])
        new_s[7].append(s_s)

    y_prompt = _final_norm(xp, final_norm.reshape(1, d), tm=tm_p).reshape(nb, seq, d)
    y_sample = _final_norm(xs, final_norm.reshape(1, d), tm=tm_s).reshape(ns, dec, d)
    return (y_prompt, y_sample, *[jnp.stack(t) for t in new_p], *[jnp.stack(t) for t in new_s])
```

```python
import functools
import math

import numpy as np
import jax
import jax.numpy as jnp
from jax import lax
from jax.experimental import pallas as pl
from jax.experimental.pallas import tpu as pltpu

F32 = jnp.float32
BF16 = jnp.bfloat16

D_MODEL = 1024
ATTN_GROUPS = ((128, 1), (512, 4), (2048, 16))
N_GROUPS = 3
A_HEADS = 4
A_HEAD_DIM = 64
A_WIDTH = A_HEADS * A_HEAD_DIM
LANES = 128
SLABS = A_WIDTH // LANES
HEADS_PER_SLAB = LANES // A_HEAD_DIM
ROW_SLABS = D_MODEL // LANES
ROPE_THETA = 10000.0
BAND_BLOCK = 128
B_HEADS = 8
B_HEAD_DIM = 128
B_WIDTH = B_HEADS * B_HEAD_DIM
CONV_W = 4
CONV_CH = 3 * B_WIDTH
PROMPT_CHUNK = 64
SAMPLE_SEQS_PER_STEP = 4
N_EXPERTS = 8
NORM_EPS = 1e-6
A_QKV = N_GROUPS * 3 * A_WIDTH
NEG = -1e30

BA_PAD = 256
MAIN_COLS = CONV_CH + 3 * D_MODEL + BA_PAD
COL_Z = CONV_CH // D_MODEL
COL_GA = COL_Z + 1
COL_GB = COL_Z + 2
COL_BA = (CONV_CH + 3 * D_MODEL) // BA_PAD

VMEM_LIMIT = 52 * 1024 * 1024


def _cparams(*sem):
    return pltpu.CompilerParams(dimension_semantics=sem, vmem_limit_bytes=VMEM_LIMIT)


def _sigmoid(x):
    return 1.0 / (1.0 + jnp.exp(-x))


def _silu(x):
    return x * _sigmoid(x)


def _nt_dot(a, b):
    return lax.dot_general(a, b, (((1,), (1,)), ((), ())), preferred_element_type=F32)


def _tn_dot(a, b):
    return lax.dot_general(a, b, (((0,), (0,)), ((), ())), preferred_element_type=F32)


def _mod_spec(per_seq, tm, tiles_per_seq):
    if per_seq:
        return pl.BlockSpec((None, 1, D_MODEL), lambda i, *_: (i // tiles_per_seq, 0, 0))
    return pl.BlockSpec((tm, D_MODEL), lambda i, *_: (i, 0))


def _rms_mod(x, nw, sc, sh):
    y = x * lax.rsqrt(jnp.mean(x * x, axis=-1, keepdims=True) + NORM_EPS) * nw
    return y * (1.0 + sc) + sh


def _mod_body(c_ref, w_ref, b_ref, o_ref):
    s = _silu(c_ref[...]).astype(BF16)
    o_ref[...] = jnp.dot(s, w_ref[...].astype(BF16), preferred_element_type=F32) + b_ref[...]


def _modulation(c_all, w_mod, b_mod):
    depth, d, n = w_mod.shape
    rows = c_all.shape[0]
    tn = 1536
    return pl.pallas_call(
        _mod_body,
        grid=(depth, n // tn),
        in_specs=[pl.BlockSpec((rows, d), lambda l, j: (0, 0)),
                  pl.BlockSpec((None, d, tn), lambda l, j: (l, 0, j)),
                  pl.BlockSpec((None, 1, tn), lambda l, j: (l, 0, j))],
        out_specs=pl.BlockSpec((None, rows, tn), lambda l, j: (l, 0, j)),
        out_shape=jax.ShapeDtypeStruct((depth, rows, n), F32),
        compiler_params=_cparams("arbitrary", "arbitrary"),
        name="adaln_mod",
    )(c_all, w_mod, b_mod.reshape(depth, 1, n))


def _proj_body(x_ref, sc_ref, sh_ref, nw_ref, w_ref, *rest, rope):
    if rope:
        cos_ref, sin_ref, o_ref, h_scr = rest
    else:
        o_ref, h_scr = rest

    @pl.when(pl.program_id(1) == 0)
    def _():
        h_scr[...] = _rms_mod(x_ref[...], nw_ref[...], sc_ref[...], sh_ref[...]).astype(BF16)

    y = jnp.dot(h_scr[...], w_ref[...], preferred_element_type=F32)
    if not rope:
        o_ref[...] = y
        return
    cos = cos_ref[...]
    sin = sin_ref[...]
    lane = lax.broadcasted_iota(jnp.int32, cos.shape, 1)
    first_half = (lane % A_HEAD_DIM) < (A_HEAD_DIM // 2)
    for c in range(3 * SLABS):
        t = y[:, c * LANES:(c + 1) * LANES]
        if c < 2 * SLABS:
            partner = jnp.where(first_half, pltpu.roll(t, LANES - A_HEAD_DIM // 2, 1),
                                pltpu.roll(t, A_HEAD_DIM // 2, 1))
            t = t * cos + partner * sin
        o_ref[c] = t


def _project(x, sc, sh, nw, w, *, tm, tn, per_seq, rope_tabs=None, name):
    rows, d = x.shape
    n = w.shape[1]
    tps = (rows // sc.shape[0]) // tm if per_seq else None
    in_specs = [pl.BlockSpec((tm, d), lambda i, j: (i, 0)),
                _mod_spec(per_seq, tm, tps), _mod_spec(per_seq, tm, tps),
                pl.BlockSpec((1, d), lambda i, j: (0, 0)),
                pl.BlockSpec((d, tn), lambda i, j: (0, j))]
    args = [x, sc, sh, nw, w]
    out_spec = pl.BlockSpec((tm, tn), lambda i, j: (i, j))
    out_shape = jax.ShapeDtypeStruct((rows, n), F32)
    if rope_tabs is not None:
        in_specs += [pl.BlockSpec((tm, LANES), lambda i, j: (i, 0))] * 2
        args += list(rope_tabs)
        out_spec = pl.BlockSpec((tn // LANES, tm, LANES), lambda i, j: (j, i, 0))
        out_shape = jax.ShapeDtypeStruct((n // LANES, rows, LANES), F32)
    return pl.pallas_call(
        functools.partial(_proj_body, rope=rope_tabs is not None),
        grid=(rows // tm, n // tn),
        in_specs=in_specs,
        out_specs=out_spec,
        out_shape=out_shape,
        scratch_shapes=[pltpu.VMEM((tm, d), BF16)],
        compiler_params=_cparams("arbitrary", "arbitrary"),
        name=name,
    )(*args)


def _head_masks(width):
    lane = lax.broadcasted_iota(jnp.int32, (1, width), 1)
    return [(lane // A_HEAD_DIM == h).astype(F32) for h in range(A_HEADS)]


def _merge_lse(o_a, l_a, o_b, l_b):
    m = jnp.maximum(l_a, l_b)
    e_a = jnp.exp(l_a - m)
    e_b = jnp.exp(l_b - m)
    den = e_a + e_b
    return (o_a * e_a + o_b * e_b) / den, m + jnp.log(den)


def _band_body(q_ref, kc_ref, kp_ref, vc_ref, vp_ref, bias_ref, *rest, dil, has_prev_group):
    if has_prev_group:
        oprev_ref, lprev_ref, o_ref, l_ref = rest
    else:
        o_ref, l_ref = rest
    blk = BAND_BLOCK
    col = lax.broadcasted_iota(jnp.int32, (blk, 2 * blk), 1)
    no_prev = jnp.logical_and(pl.program_id(1) == 0, col < blk)
    bias = jnp.where(no_prev, NEG, bias_ref[...])
    bias_h = jnp.concatenate([bias] * HEADS_PER_SLAB, axis=0)
    hm = _head_masks(LANES)

    def residue(r):
        rows = pl.ds(r, blk, stride=dil) if dil > 1 else pl.ds(0, blk)
        for sl in range(SLABS):
            q = q_ref[sl, rows, :] * (A_HEAD_DIM ** -0.5)
            k = jnp.concatenate([kp_ref[sl, rows, :], kc_ref[sl, rows, :]], axis=0).astype(BF16)
            v = jnp.concatenate([vp_ref[sl, rows, :], vc_ref[sl, rows, :]], axis=0).astype(BF16)
            qh = jnp.concatenate([q * hm[h] for h in range(HEADS_PER_SLAB)], axis=0).astype(BF16)
            s = _nt_dot(qh, k) + bias_h
            m = jnp.max(s, axis=1, keepdims=True)
            p = jnp.exp(s - m)
            l = jnp.sum(p, axis=1, keepdims=True)
            oh = jnp.dot(p.astype(BF16), v, preferred_element_type=F32) / l
            lseh = m + jnp.log(l)
            o = sum(oh[h * blk:(h + 1) * blk] * hm[h] for h in range(HEADS_PER_SLAB))
            lse = sum(lseh[h * blk:(h + 1) * blk] * hm[h] for h in range(HEADS_PER_SLAB))
            if has_prev_group:
                o, lse = _merge_lse(oprev_ref[sl, rows, :], lprev_ref[sl, rows, :], o, lse)
            o_ref[sl, rows, :] = o
            l_ref[sl, rows, :] = lse

    if dil <= 4:
        for r in range(dil):
            residue(r)
    else:
        def loop_body(r, carry):
            residue(r)
            return carry
        lax.fori_loop(0, dil, loop_body, 0)


def _band_bias(nk):
    qi = np.arange(BAND_BLOCK)[:, None]
    kj = np.arange(2 * BAND_BLOCK)[None, :]
    rel = kj - BAND_BLOCK - qi
    return jnp.asarray(np.where((rel <= 0) & (rel >= -nk), 0.0, NEG), F32)


def _band_attention(qkv, gi, seq_len, prev):
    win, dil = ATTN_GROUPS[gi]
    rows = qkv.shape[1]
    sb_rows = BAND_BLOCK * dil
    nsb = seq_len // sb_rows
    nseq = rows // seq_len

    def cur(c):
        return pl.BlockSpec((SLABS, sb_rows, LANES), lambda n, s: (c, n * nsb + s, 0))

    def prv(c):
        return pl.BlockSpec((SLABS, sb_rows, LANES), lambda n, s: (c, n * nsb + jnp.maximum(s - 1, 0), 0))

    row_spec = pl.BlockSpec((SLABS, sb_rows, LANES), lambda n, s: (0, n * nsb + s, 0))
    in_specs = [cur(3 * gi), cur(3 * gi + 1), prv(3 * gi + 1), cur(3 * gi + 2), prv(3 * gi + 2),
                pl.BlockSpec((BAND_BLOCK, 2 * BAND_BLOCK), lambda n, s: (0, 0))]
    args = [qkv, qkv, qkv, qkv, qkv, _band_bias(win // dil)]
    if prev is not None:
        in_specs += [row_spec, row_spec]
        args += list(prev)
    return pl.pallas_call(
        functools.partial(_band_body, dil=dil, has_prev_group=prev is not None),
        grid=(nseq, nsb),
        in_specs=in_specs,
        out_specs=[row_spec, row_spec],
        out_shape=[jax.ShapeDtypeStruct((SLABS, rows, LANES), F32)] * 2,
        compiler_params=_cparams("arbitrary", "arbitrary"),
        name="band_attn_g%d" % gi,
    )(*args)


def _cached_body(row_ref, *rest, n_new):
    caches = rest[:2 * N_GROUPS]
    biases = rest[2 * N_GROUPS:4 * N_GROUPS]
    o_ref = rest[4 * N_GROUPS]
    hm = _head_masks(A_WIDTH)
    o_run = l_run = None
    def slab_cols(c):
        return jnp.concatenate([row_ref[c * SLABS + sl] for sl in range(SLABS)], axis=1)

    for gi in range(N_GROUPS):
        q = slab_cols(3 * gi) * (A_HEAD_DIM ** -0.5)
        kn = slab_cols(3 * gi + 1).astype(BF16)
        vn = slab_cols(3 * gi + 2).astype(BF16)
        kc = caches[2 * gi][...].astype(BF16)
        vc = caches[2 * gi + 1][...].astype(BF16)
        q4 = jnp.concatenate([q * hm[h] for h in range(A_HEADS)], axis=0).astype(BF16)
        s_c = _nt_dot(q4, kc) + biases[2 * gi][...]
        s_n = _nt_dot(q4, kn) + biases[2 * gi + 1][...]
        m = jnp.maximum(jnp.max(s_c, axis=1, keepdims=True), jnp.max(s_n, axis=1, keepdims=True))
        p_c = jnp.exp(s_c - m)
        p_n = jnp.exp(s_n - m)
        l = jnp.sum(p_c, axis=1, keepdims=True) + jnp.sum(p_n, axis=1, keepdims=True)
        o4 = (jnp.dot(p_c.astype(BF16), vc, preferred_element_type=F32)
              + jnp.dot(p_n.astype(BF16), vn, preferred_element_type=F32)) / l
        lse4 = m + jnp.log(l)
        o = sum(o4[h * n_new:(h + 1) * n_new] * hm[h] for h in range(A_HEADS))
        lse = sum(lse4[h * n_new:(h + 1) * n_new] * hm[h] for h in range(A_HEADS))
        if gi == 0:
            o_run, l_run = o, lse
        else:
            o_run, l_run = _merge_lse(o_run, l_run, o, lse)
    for sl in range(SLABS):
        o_ref[sl] = o_run[:, sl * LANES:(sl + 1) * LANES]


def _cache_rows(wb, n_new, dil):
    res = sorted({(wb + t) % dil for t in range(n_new)})
    if wb % dil == 0 and len(res) < dil and res == list(range(res[0], res[0] + len(res))):
        return res[0], len(res)
    return 0, dil


def _compact_cache(c, r0, nr, dil):
    depth, n, wb = c.shape[:3]
    if nr == dil:
        return c.reshape(depth, n, wb, A_WIDTH)
    return c.reshape(depth, n, wb // dil, dil, A_WIDTH)[:, :, :, r0:r0 + nr].reshape(depth, n, -1, A_WIDTH)


def _cached_bias(wb, n_new, win, dil, cache_pos):
    nk = win // dil
    t = np.tile(np.arange(n_new), A_HEADS)[:, None]

    def bias(pos):
        delta = wb + t - pos[None, :]
        ok = (delta >= 0) & (delta % dil == 0) & (delta <= dil * nk)
        return jnp.asarray(np.where(ok, 0.0, NEG).astype(np.float32))

    return bias(cache_pos), bias(wb + np.arange(n_new))


def _cached_attention(qkv, caches, cache_pos, layer, n_new):
    nslab, rows, _ = qkv.shape
    nseq = rows // n_new
    in_specs = [pl.BlockSpec((nslab, n_new, LANES), lambda n: (0, n, 0))]
    args = [qkv]
    for c in caches:
        in_specs.append(pl.BlockSpec((None, None, c.shape[2], A_WIDTH), lambda n: (layer, n, 0, 0)))
        args.append(c)
    for gi, (win, dil) in enumerate(ATTN_GROUPS):
        wb, pos = cache_pos[gi]
        for b in _cached_bias(wb, n_new, win, dil, pos):
            in_specs.append(pl.BlockSpec(b.shape, lambda n: (0, 0)))
            args.append(b)
    return pl.pallas_call(
        functools.partial(_cached_body, n_new=n_new),
        grid=(nseq,),
        in_specs=in_specs,
        out_specs=pl.BlockSpec((SLABS, n_new, LANES), lambda n: (0, n, 0)),
        out_shape=jax.ShapeDtypeStruct((SLABS, rows, LANES), F32),
        compiler_params=_cparams("arbitrary"),
        name="cached_attn",
    )(*args)


def _cumsum_rows(x):
    n = x.shape[0]
    row = lax.broadcasted_iota(jnp.int32, x.shape, 0)
    sh = 1
    while sh < n:
        x = x + jnp.where(row >= sh, pltpu.roll(x, sh, 0), 0.0)
        sh *= 2
    return x


def _delta_body(u_ref, ba_ref, conv0_ref, s0_ref, cw_ref, ap_ref, o_ref, s_ref, ubuf, *, chunk, nsb):
    C = chunk
    f32dot = functools.partial(jnp.dot, preferred_element_type=F32)

    @pl.when(pl.program_id(1) == 0)
    def _():
        ubuf[:, 0:8, :] = jnp.zeros((nsb, 8, CONV_CH), F32)
        ubuf[:, 8 - (CONV_W - 1):8, :] = conv0_ref[...]
        s_ref[...] = s0_ref[...]

    ubuf[:, 8:8 + C, :] = u_ref[...]
    cw = cw_ref[...]
    ap = ap_ref[...]
    ri = lax.broadcasted_iota(jnp.int32, (C, C), 0)
    ci = lax.broadcasted_iota(jnp.int32, (C, C), 1)
    eye_f = (ri == ci).astype(F32)
    levels = []
    b = 1
    while b < C:
        levels.append(((ri // (2 * b)) == (ci // (2 * b))) & ((ri // b) % 2 == 1) & ((ci // b) % 2 == 0))
        b *= 2

    units = []
    for sq in range(nsb):
        conv = ubuf[sq, 8:8 + C, :] * cw[CONV_W - 1:CONV_W]
        for i in range(CONV_W - 1):
            off = 8 - (CONV_W - 1) + i
            conv = conv + ubuf[sq, off:off + C, :] * cw[i:i + 1]
        cu = _silu(conv)
        ba = ba_ref[sq]
        beta_f = _sigmoid(ba)
        z = ba + ap[1:2]
        softplus = jnp.maximum(z, 0.0) + jnp.log(1.0 + jnp.exp(-jnp.abs(z)))
        g_cum = _cumsum_rows(-jnp.exp(ap[0:1]) * softplus)
        g_rows = g_cum[:, 0:LANES].T
        for h in range(B_HEADS):
            lo = h * B_HEAD_DIM
            qh = cu[:, lo:lo + B_HEAD_DIM]
            kh = cu[:, B_WIDTH + lo:B_WIDTH + lo + B_HEAD_DIM]
            qn = qh * lax.rsqrt(jnp.sum(qh * qh, axis=-1, keepdims=True) + NORM_EPS) * (B_HEAD_DIM ** -0.5)
            kn = kh * lax.rsqrt(jnp.sum(kh * kh, axis=-1, keepdims=True) + NORM_EPS)
            g_col = g_cum[:, B_HEADS + h:B_HEADS + h + 1]
            units.append(dict(sq=sq, h=h, qn=qn, kn=kn, kb=kn.astype(BF16),
                              vh=cu[:, 2 * B_WIDTH + lo:2 * B_WIDTH + lo + B_HEAD_DIM],
                              beta=beta_f[:, h:h + 1], g_col=g_col,
                              g_row=g_rows[B_HEADS + h:B_HEADS + h + 1, :]))
    ubuf[:, 0:8, :] = ubuf[:, C:C + 8, :]

    for un in units:
        un["qkk"] = _nt_dot(jnp.concatenate([un["kn"], un["qn"]], axis=0).astype(BF16), un["kb"])
    for un in units:
        dm = jnp.where(ri >= ci, jnp.exp(jnp.where(ri >= ci, un["g_col"] - un["g_row"], 0.0)), 0.0)
        a = jnp.where(ri > ci, un["beta"] * un["qkk"][0:C] * dm, 0.0)
        un["qk"] = (un["qkk"][C:2 * C] * dm).astype(BF16)
        un["ab"] = a.astype(BF16)
        un["t"] = eye_f - jnp.where(levels[0], a, 0.0)
    for lvl in levels[1:]:
        for un in units:
            un["tb"] = un["t"].astype(BF16)
            un["ta"] = f32dot(un["tb"], un["ab"]).astype(BF16)
        for un in units:
            un["t"] = un["t"] - jnp.where(lvl, f32dot(un["ta"], un["tb"]), 0.0)
    for un in units:
        e_g = jnp.exp(un["g_col"])
        rhs = jnp.concatenate([un["beta"] * un["vh"], un["beta"] * e_g * un["kn"]], axis=1).astype(BF16)
        un["uw"] = f32dot(un["t"].astype(BF16), rhs)
        un["qe"] = un["qn"] * e_g
    for un in units:
        s_old = s_ref[un["sq"], un["h"]]
        lhs = jnp.concatenate([un["uw"][:, B_HEAD_DIM:], un["qe"]], axis=0).astype(BF16)
        un["ws_qs"] = f32dot(lhs, s_old.astype(BF16))
    for un in units:
        sq, h = un["sq"], un["h"]
        wvb = (un["uw"][:, :B_HEAD_DIM] - un["ws_qs"][0:C]).astype(BF16)
        g_last = un["g_col"][C - 1:C, :]
        kd = (un["kn"] * jnp.exp(g_last - un["g_col"])).astype(BF16)
        o_ref[sq, :, h * B_HEAD_DIM:(h + 1) * B_HEAD_DIM] = un["ws_qs"][C:2 * C] + f32dot(un["qk"], wvb)
        s_ref[sq, h] = jnp.exp(g_last) * s_ref[sq, h] + _tn_dot(kd, wvb)


def _delta_rule(proj_main, conv0, s0, layer, conv_w, a_par, seq_len, chunk, nsb):
    rows, cols = proj_main.shape
    nseq = rows // seq_len
    nc = seq_len // chunk
    pm3 = proj_main.reshape(nseq, seq_len, cols)
    state_dims = (B_HEADS, B_HEAD_DIM, B_HEAD_DIM)
    o_b, s_new = pl.pallas_call(
        functools.partial(_delta_body, chunk=chunk, nsb=nsb),
        grid=(nseq // nsb, nc),
        in_specs=[pl.BlockSpec((nsb, chunk, CONV_CH), lambda n, c: (n, c, 0)),
                  pl.BlockSpec((nsb, chunk, BA_PAD), lambda n, c: (n, c, COL_BA)),
                  pl.BlockSpec((None, nsb, CONV_W - 1, CONV_CH), lambda n, c: (layer, n, 0, 0)),
                  pl.BlockSpec((None, nsb) + state_dims, lambda n, c: (layer, n, 0, 0, 0)),
                  pl.BlockSpec((CONV_W, CONV_CH), lambda n, c: (0, 0)),
                  pl.BlockSpec((2, BA_PAD), lambda n, c: (0, 0))],
        out_specs=[pl.BlockSpec((nsb, chunk, B_WIDTH), lambda n, c: (n, c, 0)),
                   pl.BlockSpec((nsb,) + state_dims, lambda n, c: (n, 0, 0, 0))],
        out_shape=[jax.ShapeDtypeStruct((nseq, seq_len, B_WIDTH), F32),
                   jax.ShapeDtypeStruct((nseq,) + state_dims, F32)],
        scratch_shapes=[pltpu.VMEM((nsb, chunk + 8, CONV_CH), F32)],
        compiler_params=_cparams("arbitrary", "arbitrary"),
        name="delta_c%d" % chunk,
    )(pm3, pm3, conv0, s0, conv_w, a_par)
    return o_b.reshape(rows, B_WIDTH), s_new


def _mix_out_body(x_ref, oa_ref, ob_ref, z_ref, ga_ref, gb_ref, g1_ref, sc2_ref, sh2_ref,
                  gn_ref, nf_ref, wa_ref, wb_ref, wo_ref, xo_ref, h2_ref, *, token_major):
    oa = jnp.concatenate([oa_ref[sl] for sl in range(SLABS)], axis=1)
    ya = jnp.dot(oa.astype(BF16), wa_ref[...], preferred_element_type=F32)
    gn = gn_ref[...]
    parts = []
    for h in range(B_HEADS):
        lo = h * B_HEAD_DIM
        t = ob_ref[:, lo:lo + B_HEAD_DIM]
        t = t * lax.rsqrt(jnp.mean(t * t, axis=-1, keepdims=True) + NORM_EPS) * gn
        parts.append((t * _silu(z_ref[:, lo:lo + B_HEAD_DIM])).astype(BF16))
    yb = jnp.dot(jnp.concatenate(parts, axis=1), wb_ref[...], preferred_element_type=F32)
    merged = _sigmoid(ga_ref[...]) * ya + _sigmoid(gb_ref[...]) * yb
    mix = jnp.dot(merged.astype(BF16), wo_ref[...], preferred_element_type=F32)
    xn = x_ref[...] + g1_ref[...] * mix
    xo_ref[...] = xn
    h2 = _rms_mod(xn, nf_ref[...], sc2_ref[...], sh2_ref[...])
    if token_major:
        for s in range(ROW_SLABS):
            h2_ref[pl.ds(s, xn.shape[0], stride=ROW_SLABS), :] = h2[:, s * LANES:(s + 1) * LANES]
    else:
        h2_ref[...] = h2.astype(BF16)


def _mixer_out(x, o_a, o_b, proj_main, g1, sc2, sh2, gdn_norm, norm_ffn, wa, wb, wo, *, tm, per_seq,
               token_major):
    rows, d = x.shape
    if token_major:
        h2_spec = pl.BlockSpec((tm * ROW_SLABS, LANES), lambda i: (i, 0))
        h2_shape = jax.ShapeDtypeStruct((rows * ROW_SLABS, LANES), F32)
    else:
        h2_spec = pl.BlockSpec((tm, d), lambda i: (i, 0))
        h2_shape = jax.ShapeDtypeStruct((rows, d), BF16)
    tps = (rows // g1.shape[0]) // tm if per_seq else None
    ms = _mod_spec(per_seq, tm, tps)

    def const(a):
        return pl.BlockSpec(a.shape, lambda i: (0,) * a.ndim)

    def colblk(c):
        return pl.BlockSpec((tm, d), lambda i: (i, c))

    return pl.pallas_call(
        functools.partial(_mix_out_body, token_major=token_major),
        grid=(rows // tm,),
        in_specs=[colblk(0), pl.BlockSpec((SLABS, tm, LANES), lambda i: (0, i, 0)), colblk(0),
                  colblk(COL_Z), colblk(COL_GA), colblk(COL_GB), ms, ms, ms,
                  const(gdn_norm), const(norm_ffn), const(wa), const(wb), const(wo)],
        out_specs=[colblk(0), h2_spec],
        out_shape=[jax.ShapeDtypeStruct((rows, d), F32), h2_shape],
        compiler_params=_cparams("arbitrary"),
        name="mixer_out",
    )(x, o_a, o_b, proj_main, proj_main, proj_main, g1, sc2, sh2, gdn_norm, norm_ffn, wa, wb, wo)


def _swiglu_accumulate(h, wg_ref, wu_ref, wd_ref, acc):
    gte = jnp.dot(h, wg_ref[...], preferred_element_type=F32)
    up = jnp.dot(h, wu_ref[...], preferred_element_type=F32)
    acc[...] += jnp.dot((_silu(gte) * up).astype(BF16), wd_ref[...], preferred_element_type=F32)


def _ffn_body(h_ref, base_ref, g2_ref, wg_ref, wu_ref, wd_ref, o_ref, acc):
    j = pl.program_id(1)

    @pl.when(j == 0)
    def _():
        acc[...] = jnp.zeros_like(acc)

    _swiglu_accumulate(h_ref[...], wg_ref, wu_ref, wd_ref, acc)

    @pl.when(j == pl.num_programs(1) - 1)
    def _():
        o_ref[...] = base_ref[...] + g2_ref[...] * acc[...]


def _ffn(h2, base, g2, w_up, w_down, *, tm, tf, per_seq):
    rows, d = h2.shape
    ff = w_down.shape[0]
    nj = ff // tf
    tps = (rows // g2.shape[0]) // tm if per_seq else None
    return pl.pallas_call(
        _ffn_body,
        grid=(rows // tm, nj),
        in_specs=[pl.BlockSpec((tm, d), lambda i, j: (i, 0)),
                  pl.BlockSpec((tm, d), lambda i, j: (i, 0)),
                  _mod_spec(per_seq, tm, tps),
                  pl.BlockSpec((d, tf), lambda i, j: (0, j)),
                  pl.BlockSpec((d, tf), lambda i, j: (0, j + nj)),
                  pl.BlockSpec((tf, d), lambda i, j: (j, 0))],
        out_specs=pl.BlockSpec((tm, d), lambda i, j: (i, 0)),
        out_shape=jax.ShapeDtypeStruct((rows, d), F32),
        scratch_shapes=[pltpu.VMEM((tm, d), F32)],
        compiler_params=_cparams("arbitrary", "arbitrary"),
        name="ffn",
    )(h2, base, g2, w_up, w_up, w_down)


R_I1, R_I2, R_W1, R_W2, R_R1, R_R2 = range(6)


def _router_body(x_ref, sc_ref, sh_ref, nf_ref, wr_ref, route_ref, cnt_ref, carry):
    @pl.when(pl.program_id(0) == 0)
    def _():
        carry[...] = jnp.zeros_like(carry)

    h = _rms_mod(x_ref[...], nf_ref[...], sc_ref[...], sh_ref[...])
    logits = jnp.dot(h, wr_ref[...], preferred_element_type=F32, precision=lax.Precision.HIGHEST)
    lane = lax.broadcasted_iota(jnp.int32, logits.shape, 1).astype(F32)
    width = float(logits.shape[1])
    logits = jnp.where(lane < N_EXPERTS, logits, NEG)
    m1 = jnp.max(logits, axis=1, keepdims=True)
    i1 = jnp.min(jnp.where(logits == m1, lane, width), axis=1, keepdims=True)
    rest = jnp.where(lane == i1, NEG, logits)
    m2 = jnp.max(rest, axis=1, keepdims=True)
    i2 = jnp.min(jnp.where(rest == m2, lane, width), axis=1, keepdims=True)
    e2 = jnp.exp(m2 - m1)
    w1 = 1.0 / (1.0 + e2)
    sel = jnp.where(lane == i1, 1.0, 0.0) + jnp.where(lane == i2, 1.0, 0.0)
    incl = _cumsum_rows(sel) + carry[...]
    excl = incl - sel
    r1 = jnp.sum(jnp.where(lane == i1, excl, 0.0), axis=1, keepdims=True)
    r2 = jnp.sum(jnp.where(lane == i2, excl, 0.0), axis=1, keepdims=True)
    total = incl[incl.shape[0] - 1:, :]
    carry[...] = total
    cnt_ref[...] = total
    route = jnp.zeros_like(logits)
    for k, v in ((R_I1, i1), (R_I2, i2), (R_W1, w1), (R_W2, e2 * w1), (R_R1, r1), (R_R2, r2)):
        route = jnp.where(lane == k, v, route)
    route_ref[...] = route


def _router(x, sc2, sh2, norm_ffn, w_router_pad, *, tm, per_seq):
    rows, d = x.shape
    tps = (rows // sc2.shape[0]) // tm if per_seq else None
    ms = _mod_spec(per_seq, tm, tps)
    width = w_router_pad.shape[1]
    return pl.pallas_call(
        _router_body,
        grid=(rows // tm,),
        in_specs=[pl.BlockSpec((tm, d), lambda i: (i, 0)), ms, ms,
                  pl.BlockSpec((1, d), lambda i: (0, 0)),
                  pl.BlockSpec((d, width), lambda i: (0, 0))],
        out_specs=[pl.BlockSpec((tm, width), lambda i: (i, 0)), pl.BlockSpec((1, width), lambda i: (0, 0))],
        out_shape=[jax.ShapeDtypeStruct((rows, width), F32), jax.ShapeDtypeStruct((1, width), F32)],
        scratch_shapes=[pltpu.VMEM((1, width), F32)],
        compiler_params=_cparams("arbitrary"),
        name="router",
    )(x, sc2, sh2, norm_ffn, w_router_pad)


def _route_plan(route, counts, tm):
    rows = route.shape[0]
    n_tiles = -(-2 * rows // tm) + N_EXPERTS
    i1 = route[:, R_I1].astype(jnp.int32)
    i2 = route[:, R_I2].astype(jnp.int32)
    cnt = counts[0, :N_EXPERTS].astype(jnp.int32)
    tiles = (cnt + tm - 1) // tm
    ends = jnp.cumsum(tiles)
    starts = (ends - tiles) * tm
    pos1 = starts[i1] + route[:, R_R1].astype(jnp.int32)
    pos2 = starts[i2] + route[:, R_R2].astype(jnp.int32)
    tile_expert = jnp.minimum(jnp.sum(jnp.arange(n_tiles)[:, None] >= ends[None, :], axis=1), N_EXPERTS - 1)
    tok = jnp.arange(rows, dtype=jnp.int32)
    row_token = jnp.zeros((n_tiles * tm,), jnp.int32).at[pos1].set(tok).at[pos2].set(tok)
    return dict(n_tiles=n_tiles, tile_expert=tile_expert.astype(jnp.int32), n_used=ends[-1:].astype(jnp.int32),
                row_token=row_token.reshape(n_tiles, 1, tm), pos1=pos1, pos2=pos2)


def _start_row_gather(idx_ref, n, src_hbm, dst, sem):
    def body(r, carry):
        src = pl.multiple_of(idx_ref[0, 0, r] * ROW_SLABS, ROW_SLABS)
        pltpu.make_async_copy(src_hbm.at[pl.ds(src, ROW_SLABS), :],
                              dst.at[pl.ds(pl.multiple_of(r * ROW_SLABS, ROW_SLABS), ROW_SLABS), :], sem).start()
        return carry
    lax.fori_loop(0, n, body, 0)


def _wait_row_gather(n, src_hbm, dst, sem):
    pltpu.make_async_copy(src_hbm.at[pl.ds(0, n * ROW_SLABS), :], dst, sem).wait()


def _moe_ffn_body(te_ref, nu_ref, rt_ref, rtn_ref, h_hbm, wg_ref, wu_ref, wd_ref, o_ref,
                  xbuf, hbuf, acc, sem, *, tm):
    i = pl.program_id(0)
    j = pl.program_id(1)
    used = i < nu_ref[0]
    slot = i % 2

    @pl.when(jnp.logical_and(j == 0, used))
    def _():
        @pl.when(i == 0)
        def _():
            _start_row_gather(rt_ref, tm, h_hbm, xbuf.at[0], sem.at[0])

        _wait_row_gather(tm, h_hbm, xbuf.at[slot], sem.at[slot])

        @pl.when(i + 1 < nu_ref[0])
        def _():
            _start_row_gather(rtn_ref, tm, h_hbm, xbuf.at[1 - slot], sem.at[1 - slot])

        for s in range(ROW_SLABS):
            hbuf[:, s * LANES:(s + 1) * LANES] = xbuf[slot, pl.ds(s, tm, stride=ROW_SLABS), :].astype(BF16)
        acc[...] = jnp.zeros_like(acc)

    @pl.when(used)
    def _():
        _swiglu_accumulate(hbuf[...], wg_ref, wu_ref, wd_ref, acc)

    last = j == pl.num_programs(1) - 1

    @pl.when(jnp.logical_and(last, used))
    def _():
        for s in range(ROW_SLABS):
            o_ref[pl.ds(s, tm, stride=ROW_SLABS), :] = acc[:, s * LANES:(s + 1) * LANES]

    @pl.when(jnp.logical_and(last, jnp.logical_not(used)))
    def _():
        o_ref[...] = jnp.zeros_like(o_ref)


def _moe_ffn(h_rows, plan, w_up, w_down, moe_idx, *, tm, tf):
    d = w_up.shape[-2]
    ff = w_down.shape[-2]
    nj = ff // tf
    n_tiles = plan["n_tiles"]

    def jj(i, j, nu):
        return jnp.where(i < nu[0], j, nj - 1)

    grid_spec = pltpu.PrefetchScalarGridSpec(
        num_scalar_prefetch=2,
        grid=(n_tiles, nj),
        in_specs=[pl.BlockSpec((1, 1, tm), lambda i, j, te, nu: (i, 0, 0), memory_space=pltpu.SMEM),
                  pl.BlockSpec((1, 1, tm), lambda i, j, te, nu: (jnp.minimum(i + 1, n_tiles - 1), 0, 0),
                               memory_space=pltpu.SMEM),
                  pl.BlockSpec(memory_space=pl.ANY),
                  pl.BlockSpec((None, None, d, tf), lambda i, j, te, nu: (moe_idx, te[i], 0, jj(i, j, nu))),
                  pl.BlockSpec((None, None, d, tf), lambda i, j, te, nu: (moe_idx, te[i], 0, jj(i, j, nu) + nj)),
                  pl.BlockSpec((None, None, tf, d), lambda i, j, te, nu: (moe_idx, te[i], jj(i, j, nu), 0))],
        out_specs=pl.BlockSpec((tm * ROW_SLABS, LANES), lambda i, j, te, nu: (i, 0)),
        scratch_shapes=[pltpu.VMEM((2, tm * ROW_SLABS, LANES), F32), pltpu.VMEM((tm, d), BF16),
                        pltpu.VMEM((tm, d), F32), pltpu.SemaphoreType.DMA((2,))])
    return pl.pallas_call(
        functools.partial(_moe_ffn_body, tm=tm),
        grid_spec=grid_spec,
        out_shape=jax.ShapeDtypeStruct((n_tiles * tm * ROW_SLABS, LANES), F32),
        compiler_params=pltpu.CompilerParams(dimension_semantics=("arbitrary", "arbitrary"),
                                             vmem_limit_bytes=VMEM_LIMIT, disable_bounds_checks=True),
        name="moe_ffn",
    )(plan["tile_expert"], plan["n_used"], plan["row_token"], plan["row_token"], h_rows, w_up, w_up, w_down)


def _moe_combine_body(p1_ref, p2_ref, p1n_ref, p2n_ref, y_hbm, x_ref, g2_ref, route_ref, fw_ref, o_ref,
                      buf1, buf2, sem, *, tc, final):
    i = pl.program_id(0)
    slot = i % 2

    @pl.when(i == 0)
    def _():
        _start_row_gather(p1_ref, tc, y_hbm, buf1.at[0], sem.at[0, 0])
        _start_row_gather(p2_ref, tc, y_hbm, buf2.at[0], sem.at[1, 0])

    _wait_row_gather(tc, y_hbm, buf1.at[slot], sem.at[0, slot])
    _wait_row_gather(tc, y_hbm, buf2.at[slot], sem.at[1, slot])

    @pl.when(i + 1 < pl.num_programs(0))
    def _():
        _start_row_gather(p1n_ref, tc, y_hbm, buf1.at[1 - slot], sem.at[0, 1 - slot])
        _start_row_gather(p2n_ref, tc, y_hbm, buf2.at[1 - slot], sem.at[1, 1 - slot])

    w1 = route_ref[:, R_W1:R_W1 + 1]
    w2 = route_ref[:, R_W2:R_W2 + 1]
    pieces = []
    for s in range(ROW_SLABS):
        cols = slice(s * LANES, (s + 1) * LANES)
        y = (w1 * buf1[slot, pl.ds(s, tc, stride=ROW_SLABS), :]
             + w2 * buf2[slot, pl.ds(s, tc, stride=ROW_SLABS), :])
        pieces.append(x_ref[:, cols] + g2_ref[:, cols] * y)
    xn = jnp.concatenate(pieces, axis=1)
    if final:
        xn = xn * lax.rsqrt(jnp.mean(xn * xn, axis=-1, keepdims=True) + NORM_EPS) * fw_ref[...]
    o_ref[...] = xn


def _moe_combine(y_rows, plan, route, x, g2, final_w, *, tc, per_seq, final):
    rows, d = x.shape
    nt = rows // tc
    tps = (rows // g2.shape[0]) // tc if per_seq else None
    p1 = plan["pos1"].reshape(nt, 1, tc)
    p2 = plan["pos2"].reshape(nt, 1, tc)
    cur = pl.BlockSpec((1, 1, tc), lambda i: (i, 0, 0), memory_space=pltpu.SMEM)
    nxt = pl.BlockSpec((1, 1, tc), lambda i: (jnp.minimum(i + 1, nt - 1), 0, 0), memory_space=pltpu.SMEM)
    return pl.pallas_call(
        functools.partial(_moe_combine_body, tc=tc, final=final),
        grid=(nt,),
        in_specs=[cur, cur, nxt, nxt, pl.BlockSpec(memory_space=pl.ANY),
                  pl.BlockSpec((tc, d), lambda i: (i, 0)), _mod_spec(per_seq, tc, tps),
                  pl.BlockSpec((tc, route.shape[1]), lambda i: (i, 0)),
                  pl.BlockSpec((1, d), lambda i: (0, 0))],
        out_specs=pl.BlockSpec((tc, d), lambda i: (i, 0)),
        out_shape=jax.ShapeDtypeStruct((rows, d), F32),
        scratch_shapes=[pltpu.VMEM((2, tc * ROW_SLABS, LANES), F32), pltpu.VMEM((2, tc * ROW_SLABS, LANES), F32),
                        pltpu.SemaphoreType.DMA((2, 2))],
        compiler_params=pltpu.CompilerParams(dimension_semantics=("arbitrary",),
                                             vmem_limit_bytes=VMEM_LIMIT, disable_bounds_checks=True),
        name="moe_combine",
    )(p1, p2, p1, p2, y_rows, x, g2, route, final_w)


def _final_body(x_ref, w_ref, o_ref):
    x = x_ref[...]
    o_ref[...] = x * lax.rsqrt(jnp.mean(x * x, axis=-1, keepdims=True) + NORM_EPS) * w_ref[...]


def _final_norm(x, w, *, tm):
    rows, d = x.shape
    return pl.pallas_call(
        _final_body,
        grid=(rows // tm,),
        in_specs=[pl.BlockSpec((tm, d), lambda i: (i, 0)), pl.BlockSpec((1, d), lambda i: (0, 0))],
        out_specs=pl.BlockSpec((tm, d), lambda i: (i, 0)),
        out_shape=jax.ShapeDtypeStruct((rows, d), F32),
        compiler_params=_cparams("arbitrary"),
        name="final_norm",
    )(x, w)


def _rope_tables(pos):
    half = A_HEAD_DIM // 2
    inv = jnp.exp(-math.log(ROPE_THETA) * jnp.arange(half, dtype=F32) / half)
    ang = pos.astype(F32)[:, None] * inv[None, :]
    cos, sin = jnp.cos(ang), jnp.sin(ang)
    cos_t = jnp.tile(jnp.concatenate([cos, cos], axis=1), (1, HEADS_PER_SLAB))
    sin_t = jnp.tile(jnp.concatenate([-sin, sin], axis=1), (1, HEADS_PER_SLAB))
    return cos_t, sin_t


def _slab_rows(pa, c, nseq, seq_len, keep):
    t = pa[c * SLABS:(c + 1) * SLABS].reshape(SLABS, nseq, seq_len, HEADS_PER_SLAB, A_HEAD_DIM)
    t = t[:, :, seq_len - keep:]
    return t.transpose(1, 2, 0, 3, 4).reshape(nseq, keep, A_HEADS, A_HEAD_DIM)


def _split_w_in(w):
    o_b = A_QKV
    o_z = o_b + CONV_CH
    o_ba = o_z + B_WIDTH
    o_g = o_ba + 2 * B_HEADS
    pad = jnp.zeros((w.shape[0], BA_PAD - 2 * B_HEADS), w.dtype)
    main = jnp.concatenate([w[:, o_b:o_ba], w[:, o_g:], w[:, o_ba:o_g], pad], axis=1)
    return w[:, :A_QKV].astype(BF16), main.astype(BF16)


def kernel(x_prompt, x_sample, cache_k_g0, cache_v_g0, cache_k_g1, cache_v_g1, cache_k_g2, cache_v_g2,
           state_conv, state_delta, c_prompt, c_sample, w_mod, b_mod, norm_mix, norm_ffn, w_in, conv_w,
           a_log, dt_bias, gdn_norm, w_branch_a, w_branch_b, w_out, w_ffn_up, w_ffn_down, w_router,
           w_exp_up, w_exp_down, final_norm):
    depth = w_in.shape[0]
    nb, seq, d = x_prompt.shape
    ns, dec, _ = x_sample.shape
    past = cache_k_g2.shape[2]

    c_all = jnp.concatenate([c_prompt, c_sample, jnp.zeros((-(nb + ns) % 8, d), F32)], axis=0)
    mod_all = _modulation(c_all, w_mod, b_mod)

    caches, cache_pos = [], []
    for gi, kv in enumerate(((cache_k_g0, cache_v_g0), (cache_k_g1, cache_v_g1), (cache_k_g2, cache_v_g2))):
        wb, dil = kv[0].shape[2], ATTN_GROUPS[gi][1]
        r0, nr = _cache_rows(wb, dec, dil)
        caches += [_compact_cache(c, r0, nr, dil) for c in kv]
        pos = np.arange(wb) if nr == dil else (np.arange(wb // dil)[:, None] * dil + r0 + np.arange(nr)).reshape(-1)
        cache_pos.append((wb, pos))
    tabs_p = _rope_tables(jnp.tile(jnp.arange(seq), nb))
    tabs_s = _rope_tables(jnp.tile(past + jnp.arange(dec), ns))
    zero_conv = jnp.zeros((depth, nb, CONV_W - 1, CONV_CH), F32)
    zero_state = jnp.zeros((depth, nb, B_HEADS, B_HEAD_DIM, B_HEAD_DIM), F32)

    exp_up = w_exp_up.astype(BF16)
    exp_down = w_exp_down.astype(BF16)
    xp = x_prompt.reshape(nb * seq, d)
    xs = x_sample.reshape(ns * dec, d)
    new_p = [[] for _ in range(8)]
    new_s = [[] for _ in range(8)]
    tm_p, tm_s = 512, 512

    for l in range(depth):
        mods_p = [m.reshape(nb, 1, d) for m in jnp.split(mod_all[l, :nb], 6, axis=-1)]
        mods_s = [jnp.repeat(m, dec, axis=0) for m in jnp.split(mod_all[l, nb:nb + ns], 6, axis=-1)]
        w_attn, w_main = _split_w_in(w_in[l])
        nw_mix = norm_mix[l].reshape(1, d)
        nw_ffn = norm_ffn[l].reshape(1, d)
        a_par = jnp.zeros((2, BA_PAD), F32)
        a_par = a_par.at[0, B_HEADS:2 * B_HEADS].set(a_log[l]).at[1, B_HEADS:2 * B_HEADS].set(dt_bias[l])
        wa = w_branch_a[l].astype(BF16)
        wb = w_branch_b[l].astype(BF16)
        wo = w_out[l].astype(BF16)
        gn = gdn_norm[l].reshape(1, B_HEAD_DIM)

        outs = []
        for (x, mods, per_seq, tm, tabs, is_prompt) in ((xp, mods_p, True, tm_p, tabs_p, True),
                                                        (xs, mods_s, False, tm_s, tabs_s, False)):
            sh1, sc1, g1, sh2, sc2, g2 = mods
            pm = _project(x, sc1, sh1, nw_mix, w_main, tm=tm, tn=MAIN_COLS // 5, per_seq=per_seq,
                          name="proj_main")
            pa = _project(x, sc1, sh1, nw_mix, w_attn, tm=tm, tn=3 * A_WIDTH, per_seq=per_seq,
                          rope_tabs=tabs, name="proj_attn")
            if is_prompt:
                run = None
                for gi in range(N_GROUPS):
                    run = _band_attention(pa, gi, seq, run)
                o_a = run[0]
                o_b, s_new = _delta_rule(pm, zero_conv, zero_state, l, conv_w[l], a_par, seq, PROMPT_CHUNK, nb)
            else:
                o_a = _cached_attention(pa, caches, cache_pos, l, dec)
                o_b, s_new = _delta_rule(pm, state_conv, state_delta, l, conv_w[l], a_par, dec, dec, SAMPLE_SEQS_PER_STEP)
            x, h2 = _mixer_out(x, o_a, o_b, pm, g1, sc2, sh2, gn, nw_ffn, wa, wb, wo, tm=tm, per_seq=per_seq,
                               token_major=l % 2 == 1)
            if l % 2 == 0:
                x = _ffn(h2, x, g2, w_ffn_up[l // 2].astype(BF16), w_ffn_down[l // 2].astype(BF16),
                         tm=tm, tf=1408, per_seq=per_seq)
            else:
                wr = jnp.zeros((d, LANES), F32).at[:, :N_EXPERTS].set(w_router[l // 2])
                route, counts = _router(x, sc2, sh2, nw_ffn, wr, tm=tm, per_seq=per_seq)
                plan = _route_plan(route, counts, 512 if is_prompt else 256)
                y_rows = _moe_ffn(h2, plan, exp_up, exp_down, l // 2, tm=512 if is_prompt else 256, tf=896)
                x = _moe_combine(y_rows, plan, route, x, g2, final_norm.reshape(1, d), tc=256,
                                 per_seq=per_seq, final=l == depth - 1)
            outs.append((x, pa, pm, s_new))

        (xp, pa_p, pm_p, s_p), (xs, pa_s, pm_s, s_s) = outs
        for gi, (win, dil) in enumerate(ATTN_GROUPS):
            keep = min(win, seq)
            new_p[2 * gi].append(_slab_rows(pa_p, 3 * gi + 1, nb, seq, keep))
            new_p[2 * gi + 1].append(_slab_rows(pa_p, 3 * gi + 2, nb, seq, keep))
            new_s[2 * gi].append(_slab_rows(pa_s, 3 * gi + 1, ns, dec, dec))
            new_s[2 * gi + 1].append(_slab_rows(pa_s, 3 * gi + 2, ns, dec, dec))
        new_p[6].append(pm_p.reshape(nb, seq, MAIN_COLS)[:, seq - (CONV_W - 1):, :CONV_CH])
        new_p[7].append(s_p)
        conv_s = jnp.concatenate([state_conv[l], pm_s.reshape(ns, dec, MAIN_COLS)[:, :, :CONV_CH]], axis=1)
        new_s[6].append(conv_s[:, -(CONV_W - 1):])
        new_s[7].append(s_s)

    if depth % 2 == 1:
        xp = _final_norm(xp, final_norm.reshape(1, d), tm=tm_p)
        xs = _final_norm(xs, final_norm.reshape(1, d), tm=tm_s)
    y_prompt = xp.reshape(nb, seq, d)
    y_sample = xs.reshape(ns, dec, d)
    return (y_prompt, y_sample, *[jnp.stack(t) for t in new_p], *[jnp.stack(t) for t in new_s])
```

```python
import functools
import math

import numpy as np
import jax
import jax.numpy as jnp
from jax import lax
from jax.experimental import pallas as pl
from jax.experimental.pallas import tpu as pltpu

F32 = jnp.float32
BF16 = jnp.bfloat16

D_MODEL = 1024
ATTN_GROUPS = ((128, 1), (512, 4), (2048, 16))
N_GROUPS = 3
A_HEADS = 4
A_HEAD_DIM = 64
A_WIDTH = A_HEADS * A_HEAD_DIM
LANES = 128
SLABS = A_WIDTH // LANES
HEADS_PER_SLAB = LANES // A_HEAD_DIM
ROW_SLABS = D_MODEL // LANES
ROPE_THETA = 10000.0
BAND_BLOCK = 128
B_HEADS = 8
B_HEAD_DIM = 128
B_WIDTH = B_HEADS * B_HEAD_DIM
CONV_W = 4
CONV_CH = 3 * B_WIDTH
PROMPT_CHUNK = 64
SAMPLE_SEQS_PER_STEP = 4
PROJ_ROWS = 1024
N_EXPERTS = 8
NORM_EPS = 1e-6
A_QKV = N_GROUPS * 3 * A_WIDTH
NEG = -1e30

BA_PAD = 256
MAIN_COLS = CONV_CH + 3 * D_MODEL + BA_PAD
COL_Z = CONV_CH // D_MODEL
COL_GA = COL_Z + 1
COL_GB = COL_Z + 2
COL_BA = (CONV_CH + 3 * D_MODEL) // BA_PAD

VMEM_LIMIT = 52 * 1024 * 1024


def _cparams(*sem):
    return pltpu.CompilerParams(dimension_semantics=sem, vmem_limit_bytes=VMEM_LIMIT)


def _sigmoid(x):
    return 1.0 / (1.0 + jnp.exp(-x))


def _silu(x):
    return x * _sigmoid(x)


def _nt_dot(a, b):
    return lax.dot_general(a, b, (((1,), (1,)), ((), ())), preferred_element_type=F32)


def _tn_dot(a, b):
    return lax.dot_general(a, b, (((0,), (0,)), ((), ())), preferred_element_type=F32)


def _mod_spec(per_seq, tm, tiles_per_seq):
    if per_seq:
        return pl.BlockSpec((None, 1, D_MODEL), lambda i, *_: (i // tiles_per_seq, 0, 0))
    return pl.BlockSpec((tm, D_MODEL), lambda i, *_: (i, 0))


def _rms_mod(x, nw, sc, sh):
    y = x * lax.rsqrt(jnp.mean(x * x, axis=-1, keepdims=True) + NORM_EPS) * nw
    return y * (1.0 + sc) + sh


def _mod_body(c_ref, w_ref, b_ref, o_ref):
    s = _silu(c_ref[...]).astype(BF16)
    o_ref[...] = jnp.dot(s, w_ref[...].astype(BF16), preferred_element_type=F32) + b_ref[...]


def _modulation(c_all, w_mod, b_mod):
    depth, d, n = w_mod.shape
    rows = c_all.shape[0]
    tn = 1536
    return pl.pallas_call(
        _mod_body,
        grid=(depth, n // tn),
        in_specs=[pl.BlockSpec((rows, d), lambda l, j: (0, 0)),
                  pl.BlockSpec((None, d, tn), lambda l, j: (l, 0, j)),
                  pl.BlockSpec((None, 1, tn), lambda l, j: (l, 0, j))],
        out_specs=pl.BlockSpec((None, rows, tn), lambda l, j: (l, 0, j)),
        out_shape=jax.ShapeDtypeStruct((depth, rows, n), F32),
        compiler_params=_cparams("arbitrary", "arbitrary"),
        name="adaln_mod",
    )(c_all, w_mod, b_mod.reshape(depth, 1, n))


def _proj_body(x_ref, sc_ref, sh_ref, nw_ref, w_ref, *rest, rope):
    if rope:
        cos_ref, sin_ref, o_ref, h_scr = rest
    else:
        o_ref, h_scr = rest

    @pl.when(pl.program_id(1) == 0)
    def _():
        h_scr[...] = _rms_mod(x_ref[...], nw_ref[...], sc_ref[...], sh_ref[...]).astype(BF16)

    y = jnp.dot(h_scr[...], w_ref[...], preferred_element_type=F32)
    if not rope:
        o_ref[...] = y
        return
    cos = cos_ref[...]
    sin = sin_ref[...]
    lane = lax.broadcasted_iota(jnp.int32, cos.shape, 1)
    first_half = (lane % A_HEAD_DIM) < (A_HEAD_DIM // 2)
    for c in range(3 * SLABS):
        t = y[:, c * LANES:(c + 1) * LANES]
        if c < 2 * SLABS:
            partner = jnp.where(first_half, pltpu.roll(t, LANES - A_HEAD_DIM // 2, 1),
                                pltpu.roll(t, A_HEAD_DIM // 2, 1))
            t = t * cos + partner * sin
        o_ref[c] = t


def _project(x, sc, sh, nw, w, *, tm, tn, per_seq, rope_tabs=None, name):
    rows, d = x.shape
    n = w.shape[1]
    tps = (rows // sc.shape[0]) // tm if per_seq else None
    in_specs = [pl.BlockSpec((tm, d), lambda i, j: (i, 0)),
                _mod_spec(per_seq, tm, tps), _mod_spec(per_seq, tm, tps),
                pl.BlockSpec((1, d), lambda i, j: (0, 0)),
                pl.BlockSpec((d, tn), lambda i, j: (0, j))]
    args = [x, sc, sh, nw, w]
    out_spec = pl.BlockSpec((tm, tn), lambda i, j: (i, j))
    out_shape = jax.ShapeDtypeStruct((rows, n), F32)
    if rope_tabs is not None:
        in_specs += [pl.BlockSpec((tm, LANES), lambda i, j: (i, 0))] * 2
        args += list(rope_tabs)
        out_spec = pl.BlockSpec((tn // LANES, tm, LANES), lambda i, j: (j, i, 0))
        out_shape = jax.ShapeDtypeStruct((n // LANES, rows, LANES), F32)
    return pl.pallas_call(
        functools.partial(_proj_body, rope=rope_tabs is not None),
        grid=(rows // tm, n // tn),
        in_specs=in_specs,
        out_specs=out_spec,
        out_shape=out_shape,
        scratch_shapes=[pltpu.VMEM((tm, d), BF16)],
        compiler_params=_cparams("arbitrary", "arbitrary"),
        name=name,
    )(*args)


def _head_masks(width):
    lane = lax.broadcasted_iota(jnp.int32, (1, width), 1)
    return [(lane // A_HEAD_DIM == h).astype(F32) for h in range(A_HEADS)]


def _merge_lse(o_a, l_a, o_b, l_b):
    m = jnp.maximum(l_a, l_b)
    e_a = jnp.exp(l_a - m)
    e_b = jnp.exp(l_b - m)
    den = e_a + e_b
    return (o_a * e_a + o_b * e_b) / den, m + jnp.log(den)


def _band_body(q_ref, kc_ref, kp_ref, vc_ref, vp_ref, bias_ref, *rest, dil, has_prev_group):
    if has_prev_group:
        oprev_ref, lprev_ref, o_ref, l_ref = rest
    else:
        o_ref, l_ref = rest
    blk = BAND_BLOCK
    col = lax.broadcasted_iota(jnp.int32, (blk, 2 * blk), 1)
    no_prev = jnp.logical_and(pl.program_id(1) == 0, col < blk)
    bias = jnp.where(no_prev, NEG, bias_ref[...])
    bias_h = jnp.concatenate([bias] * HEADS_PER_SLAB, axis=0)
    hm = _head_masks(LANES)

    def residue(r):
        rows = pl.ds(r, blk, stride=dil) if dil > 1 else pl.ds(0, blk)
        for sl in range(SLABS):
            q = q_ref[sl, rows, :] * (A_HEAD_DIM ** -0.5)
            k = jnp.concatenate([kp_ref[sl, rows, :], kc_ref[sl, rows, :]], axis=0).astype(BF16)
            v = jnp.concatenate([vp_ref[sl, rows, :], vc_ref[sl, rows, :]], axis=0).astype(BF16)
            qh = jnp.concatenate([q * hm[h] for h in range(HEADS_PER_SLAB)], axis=0).astype(BF16)
            s = _nt_dot(qh, k) + bias_h
            m = jnp.max(s, axis=1, keepdims=True)
            p = jnp.exp(s - m)
            l = jnp.sum(p, axis=1, keepdims=True)
            oh = jnp.dot(p.astype(BF16), v, preferred_element_type=F32) / l
            lseh = m + jnp.log(l)
            o = sum(oh[h * blk:(h + 1) * blk] * hm[h] for h in range(HEADS_PER_SLAB))
            lse = sum(lseh[h * blk:(h + 1) * blk] * hm[h] for h in range(HEADS_PER_SLAB))
            if has_prev_group:
                o, lse = _merge_lse(oprev_ref[sl, rows, :], lprev_ref[sl, rows, :], o, lse)
            o_ref[sl, rows, :] = o
            l_ref[sl, rows, :] = lse

    if dil <= 4:
        for r in range(dil):
            residue(r)
    else:
        def loop_body(r, carry):
            residue(r)
            return carry
        lax.fori_loop(0, dil, loop_body, 0)


def _band_bias(nk):
    qi = np.arange(BAND_BLOCK)[:, None]
    kj = np.arange(2 * BAND_BLOCK)[None, :]
    rel = kj - BAND_BLOCK - qi
    return jnp.asarray(np.where((rel <= 0) & (rel >= -nk), 0.0, NEG), F32)


def _band_attention(qkv, gi, seq_len, prev):
    win, dil = ATTN_GROUPS[gi]
    rows = qkv.shape[1]
    sb_rows = BAND_BLOCK * dil
    nsb = seq_len // sb_rows
    nseq = rows // seq_len

    def cur(c):
        return pl.BlockSpec((SLABS, sb_rows, LANES), lambda n, s: (c, n * nsb + s, 0))

    def prv(c):
        return pl.BlockSpec((SLABS, sb_rows, LANES), lambda n, s: (c, n * nsb + jnp.maximum(s - 1, 0), 0))

    row_spec = pl.BlockSpec((SLABS, sb_rows, LANES), lambda n, s: (0, n * nsb + s, 0))
    in_specs = [cur(3 * gi), cur(3 * gi + 1), prv(3 * gi + 1), cur(3 * gi + 2), prv(3 * gi + 2),
                pl.BlockSpec((BAND_BLOCK, 2 * BAND_BLOCK), lambda n, s: (0, 0))]
    args = [qkv, qkv, qkv, qkv, qkv, _band_bias(win // dil)]
    if prev is not None:
        in_specs += [row_spec, row_spec]
        args += list(prev)
    return pl.pallas_call(
        functools.partial(_band_body, dil=dil, has_prev_group=prev is not None),
        grid=(nseq, nsb),
        in_specs=in_specs,
        out_specs=[row_spec, row_spec],
        out_shape=[jax.ShapeDtypeStruct((SLABS, rows, LANES), F32)] * 2,
        compiler_params=_cparams("arbitrary", "arbitrary"),
        name="band_attn_g%d" % gi,
    )(*args)


def _cached_body(row_ref, *rest, n_new):
    caches = rest[:2 * N_GROUPS]
    biases = rest[2 * N_GROUPS:4 * N_GROUPS]
    o_ref = rest[4 * N_GROUPS]
    hm = _head_masks(LANES)
    for sl in range(SLABS):
        o_run = l_run = None
        for gi in range(N_GROUPS):
            q = row_ref[(3 * gi) * SLABS + sl] * (A_HEAD_DIM ** -0.5)
            kn = row_ref[(3 * gi + 1) * SLABS + sl].astype(BF16)
            vn = row_ref[(3 * gi + 2) * SLABS + sl].astype(BF16)
            kct = caches[2 * gi][sl].astype(BF16)
            vct = caches[2 * gi + 1][sl].astype(BF16)
            qh = jnp.concatenate([q * hm[h] for h in range(HEADS_PER_SLAB)], axis=0).astype(BF16)
            s_c = jnp.dot(qh, kct, preferred_element_type=F32) + biases[2 * gi][...]
            s_n = _nt_dot(qh, kn) + biases[2 * gi + 1][...]
            m = jnp.maximum(jnp.max(s_c, axis=1, keepdims=True), jnp.max(s_n, axis=1, keepdims=True))
            p_c = jnp.exp(s_c - m)
            p_n = jnp.exp(s_n - m)
            l = jnp.sum(p_c, axis=1, keepdims=True) + jnp.sum(p_n, axis=1, keepdims=True)
            oh = (_nt_dot(p_c.astype(BF16), vct) + jnp.dot(p_n.astype(BF16), vn, preferred_element_type=F32)) / l
            lseh = m + jnp.log(l)
            o = sum(oh[h * n_new:(h + 1) * n_new] * hm[h] for h in range(HEADS_PER_SLAB))
            lse = sum(lseh[h * n_new:(h + 1) * n_new] * hm[h] for h in range(HEADS_PER_SLAB))
            if gi == 0:
                o_run, l_run = o, lse
            else:
                o_run, l_run = _merge_lse(o_run, l_run, o, lse)
        o_ref[sl] = o_run


def _cached_bias(wb, n_new, win, dil):
    nk = win // dil
    t = np.tile(np.arange(n_new), HEADS_PER_SLAB)[:, None]
    delta = wb + t - np.arange(wb + n_new)[None, :]
    ok = (delta >= 0) & (delta % dil == 0) & (delta <= dil * nk)
    b = np.where(ok, 0.0, NEG).astype(np.float32)
    return jnp.asarray(b[:, :wb]), jnp.asarray(b[:, wb:])


def _window_minor(c):
    depth, n, wb = c.shape[:3]
    return c.transpose(0, 1, 3, 4, 2).reshape(depth, n, SLABS, LANES, wb)


def _cached_attention(qkv, caches, layer, n_new):
    nslab, rows, _ = qkv.shape
    nseq = rows // n_new
    in_specs = [pl.BlockSpec((nslab, n_new, LANES), lambda n: (0, n, 0))]
    args = [qkv]
    for c in caches:
        in_specs.append(pl.BlockSpec((None, None, SLABS, LANES, c.shape[4]), lambda n: (layer, n, 0, 0, 0)))
        args.append(c)
    for gi, (win, dil) in enumerate(ATTN_GROUPS):
        for b in _cached_bias(caches[2 * gi].shape[4], n_new, win, dil):
            in_specs.append(pl.BlockSpec(b.shape, lambda n: (0, 0)))
            args.append(b)
    return pl.pallas_call(
        functools.partial(_cached_body, n_new=n_new),
        grid=(nseq,),
        in_specs=in_specs,
        out_specs=pl.BlockSpec((SLABS, n_new, LANES), lambda n: (0, n, 0)),
        out_shape=jax.ShapeDtypeStruct((SLABS, rows, LANES), F32),
        compiler_params=_cparams("arbitrary"),
        name="cached_attn",
    )(*args)


def _cumsum_rows(x):
    n = x.shape[0]
    row = lax.broadcasted_iota(jnp.int32, x.shape, 0)
    sh = 1
    while sh < n:
        x = x + jnp.where(row >= sh, pltpu.roll(x, sh, 0), 0.0)
        sh *= 2
    return x


def _delta_body(u_ref, ba_ref, conv0_ref, s0_ref, cw_ref, ap_ref, o_ref, s_ref, ubuf, *, chunk, nsb):
    C = chunk
    f32dot = functools.partial(jnp.dot, preferred_element_type=F32)

    @pl.when(pl.program_id(1) == 0)
    def _():
        ubuf[:, 0:8, :] = jnp.zeros((nsb, 8, CONV_CH), F32)
        ubuf[:, 8 - (CONV_W - 1):8, :] = conv0_ref[...]
        s_ref[...] = s0_ref[...]

    ubuf[:, 8:8 + C, :] = u_ref[...]
    cw = cw_ref[...]
    ap = ap_ref[...]
    ri = lax.broadcasted_iota(jnp.int32, (C, C), 0)
    ci = lax.broadcasted_iota(jnp.int32, (C, C), 1)
    eye_f = (ri == ci).astype(F32)
    levels = []
    b = 1
    while b < C:
        levels.append(((ri // (2 * b)) == (ci // (2 * b))) & ((ri // b) % 2 == 1) & ((ci // b) % 2 == 0))
        b *= 2

    units = []
    for sq in range(nsb):
        conv = ubuf[sq, 8:8 + C, :] * cw[CONV_W - 1:CONV_W]
        for i in range(CONV_W - 1):
            off = 8 - (CONV_W - 1) + i
            conv = conv + ubuf[sq, off:off + C, :] * cw[i:i + 1]
        cu = _silu(conv)
        ba = ba_ref[sq]
        beta_f = _sigmoid(ba)
        z = ba + ap[1:2]
        softplus = jnp.maximum(z, 0.0) + jnp.log(1.0 + jnp.exp(-jnp.abs(z)))
        g_cum = _cumsum_rows(-jnp.exp(ap[0:1]) * softplus)
        g_rows = g_cum[:, 0:LANES].T
        for h in range(B_HEADS):
            lo = h * B_HEAD_DIM
            qh = cu[:, lo:lo + B_HEAD_DIM]
            kh = cu[:, B_WIDTH + lo:B_WIDTH + lo + B_HEAD_DIM]
            qn = qh * lax.rsqrt(jnp.sum(qh * qh, axis=-1, keepdims=True) + NORM_EPS) * (B_HEAD_DIM ** -0.5)
            kn = kh * lax.rsqrt(jnp.sum(kh * kh, axis=-1, keepdims=True) + NORM_EPS)
            g_col = g_cum[:, B_HEADS + h:B_HEADS + h + 1]
            units.append(dict(sq=sq, h=h, qn=qn, kn=kn, kb=kn.astype(BF16),
                              vh=cu[:, 2 * B_WIDTH + lo:2 * B_WIDTH + lo + B_HEAD_DIM],
                              beta=beta_f[:, h:h + 1], g_col=g_col,
                              g_row=g_rows[B_HEADS + h:B_HEADS + h + 1, :]))
    ubuf[:, 0:8, :] = ubuf[:, C:C + 8, :]

    for un in units:
        un["qkk"] = _nt_dot(jnp.concatenate([un["kn"], un["qn"]], axis=0).astype(BF16), un["kb"])
    for un in units:
        dm = jnp.where(ri >= ci, jnp.exp(jnp.where(ri >= ci, un["g_col"] - un["g_row"], 0.0)), 0.0)
        a = jnp.where(ri > ci, un["beta"] * un["qkk"][0:C] * dm, 0.0)
        un["qk"] = (un["qkk"][C:2 * C] * dm).astype(BF16)
        un["ab"] = a.astype(BF16)
        un["t"] = eye_f - jnp.where(levels[0], a, 0.0)
    for lvl in levels[1:]:
        for un in units:
            un["tb"] = un["t"].astype(BF16)
            un["ta"] = f32dot(un["tb"], un["ab"]).astype(BF16)
        for un in units:
            un["t"] = un["t"] - jnp.where(lvl, f32dot(un["ta"], un["tb"]), 0.0)
    for un in units:
        e_g = jnp.exp(un["g_col"])
        rhs = jnp.concatenate([un["beta"] * un["vh"], un["beta"] * e_g * un["kn"]], axis=1).astype(BF16)
        un["uw"] = f32dot(un["t"].astype(BF16), rhs)
        un["qe"] = un["qn"] * e_g
    for un in units:
        s_old = s_ref[un["sq"], un["h"]]
        lhs = jnp.concatenate([un["uw"][:, B_HEAD_DIM:], un["qe"]], axis=0).astype(BF16)
        un["ws_qs"] = f32dot(lhs, s_old.astype(BF16))
    for un in units:
        sq, h = un["sq"], un["h"]
        wvb = (un["uw"][:, :B_HEAD_DIM] - un["ws_qs"][0:C]).astype(BF16)
        g_last = un["g_col"][C - 1:C, :]
        kd = (un["kn"] * jnp.exp(g_last - un["g_col"])).astype(BF16)
        o_ref[sq, :, h * B_HEAD_DIM:(h + 1) * B_HEAD_DIM] = un["ws_qs"][C:2 * C] + f32dot(un["qk"], wvb)
        s_ref[sq, h] = jnp.exp(g_last) * s_ref[sq, h] + _tn_dot(kd, wvb)


def _delta_rule(proj_main, conv0, s0, layer, conv_w, a_par, seq_len, chunk, nsb):
    rows, cols = proj_main.shape
    nseq = rows // seq_len
    nc = seq_len // chunk
    pm3 = proj_main.reshape(nseq, seq_len, cols)
    state_dims = (B_HEADS, B_HEAD_DIM, B_HEAD_DIM)
    o_b, s_new = pl.pallas_call(
        functools.partial(_delta_body, chunk=chunk, nsb=nsb),
        grid=(nseq // nsb, nc),
        in_specs=[pl.BlockSpec((nsb, chunk, CONV_CH), lambda n, c: (n, c, 0)),
                  pl.BlockSpec((nsb, chunk, BA_PAD), lambda n, c: (n, c, COL_BA)),
                  pl.BlockSpec((None, nsb, CONV_W - 1, CONV_CH), lambda n, c: (layer, n, 0, 0)),
                  pl.BlockSpec((None, nsb) + state_dims, lambda n, c: (layer, n, 0, 0, 0)),
                  pl.BlockSpec((CONV_W, CONV_CH), lambda n, c: (0, 0)),
                  pl.BlockSpec((2, BA_PAD), lambda n, c: (0, 0))],
        out_specs=[pl.BlockSpec((nsb, chunk, B_WIDTH), lambda n, c: (n, c, 0)),
                   pl.BlockSpec((nsb,) + state_dims, lambda n, c: (n, 0, 0, 0))],
        out_shape=[jax.ShapeDtypeStruct((nseq, seq_len, B_WIDTH), F32),
                   jax.ShapeDtypeStruct((nseq,) + state_dims, F32)],
        scratch_shapes=[pltpu.VMEM((nsb, chunk + 8, CONV_CH), F32)],
        compiler_params=_cparams("arbitrary", "arbitrary"),
        name="delta_c%d" % chunk,
    )(pm3, pm3, conv0, s0, conv_w, a_par)
    return o_b.reshape(rows, B_WIDTH), s_new


def _mix_out_body(x_ref, oa_ref, ob_ref, z_ref, ga_ref, gb_ref, g1_ref, sc2_ref, sh2_ref,
                  gn_ref, nf_ref, wa_ref, wb_ref, wo_ref, xo_ref, h2_ref, *, token_major):
    oa = jnp.concatenate([oa_ref[sl] for sl in range(SLABS)], axis=1)
    ya = jnp.dot(oa.astype(BF16), wa_ref[...], preferred_element_type=F32)
    gn = gn_ref[...]
    parts = []
    for h in range(B_HEADS):
        lo = h * B_HEAD_DIM
        t = ob_ref[:, lo:lo + B_HEAD_DIM]
        t = t * lax.rsqrt(jnp.mean(t * t, axis=-1, keepdims=True) + NORM_EPS) * gn
        parts.append((t * _silu(z_ref[:, lo:lo + B_HEAD_DIM])).astype(BF16))
    yb = jnp.dot(jnp.concatenate(parts, axis=1), wb_ref[...], preferred_element_type=F32)
    merged = _sigmoid(ga_ref[...]) * ya + _sigmoid(gb_ref[...]) * yb
    mix = jnp.dot(merged.astype(BF16), wo_ref[...], preferred_element_type=F32)
    xn = x_ref[...] + g1_ref[...] * mix
    xo_ref[...] = xn
    h2 = _rms_mod(xn, nf_ref[...], sc2_ref[...], sh2_ref[...])
    if token_major:
        for s in range(ROW_SLABS):
            h2_ref[pl.ds(s, xn.shape[0], stride=ROW_SLABS), :] = h2[:, s * LANES:(s + 1) * LANES]
    else:
        h2_ref[...] = h2.astype(BF16)


def _mixer_out(x, o_a, o_b, proj_main, g1, sc2, sh2, gdn_norm, norm_ffn, wa, wb, wo, *, tm, per_seq,
               token_major):
    rows, d = x.shape
    if token_major:
        h2_spec = pl.BlockSpec((tm * ROW_SLABS, LANES), lambda i: (i, 0))
        h2_shape = jax.ShapeDtypeStruct((rows * ROW_SLABS, LANES), F32)
    else:
        h2_spec = pl.BlockSpec((tm, d), lambda i: (i, 0))
        h2_shape = jax.ShapeDtypeStruct((rows, d), BF16)
    tps = (rows // g1.shape[0]) // tm if per_seq else None
    ms = _mod_spec(per_seq, tm, tps)

    def const(a):
        return pl.BlockSpec(a.shape, lambda i: (0,) * a.ndim)

    def colblk(c):
        return pl.BlockSpec((tm, d), lambda i: (i, c))

    return pl.pallas_call(
        functools.partial(_mix_out_body, token_major=token_major),
        grid=(rows // tm,),
        in_specs=[colblk(0), pl.BlockSpec((SLABS, tm, LANES), lambda i: (0, i, 0)), colblk(0),
                  colblk(COL_Z), colblk(COL_GA), colblk(COL_GB), ms, ms, ms,
                  const(gdn_norm), const(norm_ffn), const(wa), const(wb), const(wo)],
        out_specs=[colblk(0), h2_spec],
        out_shape=[jax.ShapeDtypeStruct((rows, d), F32), h2_shape],
        compiler_params=_cparams("arbitrary"),
        name="mixer_out",
    )(x, o_a, o_b, proj_main, proj_main, proj_main, g1, sc2, sh2, gdn_norm, norm_ffn, wa, wb, wo)


def _swiglu_accumulate(h, wg_ref, wu_ref, wd_ref, acc):
    gte = jnp.dot(h, wg_ref[...], preferred_element_type=F32)
    up = jnp.dot(h, wu_ref[...], preferred_element_type=F32)
    acc[...] += jnp.dot((_silu(gte) * up).astype(BF16), wd_ref[...], preferred_element_type=F32)


def _ffn_body(h_ref, base_ref, g2_ref, wg_ref, wu_ref, wd_ref, o_ref, acc):
    j = pl.program_id(1)

    @pl.when(j == 0)
    def _():
        acc[...] = jnp.zeros_like(acc)

    _swiglu_accumulate(h_ref[...], wg_ref, wu_ref, wd_ref, acc)

    @pl.when(j == pl.num_programs(1) - 1)
    def _():
        o_ref[...] = base_ref[...] + g2_ref[...] * acc[...]


def _ffn(h2, base, g2, w_up, w_down, *, tm, tf, per_seq):
    rows, d = h2.shape
    ff = w_down.shape[0]
    nj = ff // tf
    tps = (rows // g2.shape[0]) // tm if per_seq else None
    return pl.pallas_call(
        _ffn_body,
        grid=(rows // tm, nj),
        in_specs=[pl.BlockSpec((tm, d), lambda i, j: (i, 0)),
                  pl.BlockSpec((tm, d), lambda i, j: (i, 0)),
                  _mod_spec(per_seq, tm, tps),
                  pl.BlockSpec((d, tf), lambda i, j: (0, j)),
                  pl.BlockSpec((d, tf), lambda i, j: (0, j + nj)),
                  pl.BlockSpec((tf, d), lambda i, j: (j, 0))],
        out_specs=pl.BlockSpec((tm, d), lambda i, j: (i, 0)),
        out_shape=jax.ShapeDtypeStruct((rows, d), F32),
        scratch_shapes=[pltpu.VMEM((tm, d), F32)],
        compiler_params=_cparams("arbitrary", "arbitrary"),
        name="ffn",
    )(h2, base, g2, w_up, w_up, w_down)


R_I1, R_I2, R_W1, R_W2, R_R1, R_R2 = range(6)


def _router_body(x_ref, sc_ref, sh_ref, nf_ref, wr_ref, route_ref, cnt_ref, carry):
    @pl.when(pl.program_id(0) == 0)
    def _():
        carry[...] = jnp.zeros_like(carry)

    h = _rms_mod(x_ref[...], nf_ref[...], sc_ref[...], sh_ref[...])
    logits = jnp.dot(h, wr_ref[...], preferred_element_type=F32, precision=lax.Precision.HIGHEST)
    lane = lax.broadcasted_iota(jnp.int32, logits.shape, 1).astype(F32)
    width = float(logits.shape[1])
    logits = jnp.where(lane < N_EXPERTS, logits, NEG)
    m1 = jnp.max(logits, axis=1, keepdims=True)
    i1 = jnp.min(jnp.where(logits == m1, lane, width), axis=1, keepdims=True)
    rest = jnp.where(lane == i1, NEG, logits)
    m2 = jnp.max(rest, axis=1, keepdims=True)
    i2 = jnp.min(jnp.where(rest == m2, lane, width), axis=1, keepdims=True)
    e2 = jnp.exp(m2 - m1)
    w1 = 1.0 / (1.0 + e2)
    sel = jnp.where(lane == i1, 1.0, 0.0) + jnp.where(lane == i2, 1.0, 0.0)
    incl = _cumsum_rows(sel) + carry[...]
    excl = incl - sel
    r1 = jnp.sum(jnp.where(lane == i1, excl, 0.0), axis=1, keepdims=True)
    r2 = jnp.sum(jnp.where(lane == i2, excl, 0.0), axis=1, keepdims=True)
    total = incl[incl.shape[0] - 1:, :]
    carry[...] = total
    cnt_ref[...] = total
    route = jnp.zeros_like(logits)
    for k, v in ((R_I1, i1), (R_I2, i2), (R_W1, w1), (R_W2, e2 * w1), (R_R1, r1), (R_R2, r2)):
        route = jnp.where(lane == k, v, route)
    route_ref[...] = route


def _router(x, sc2, sh2, norm_ffn, w_router_pad, *, tm, per_seq):
    rows, d = x.shape
    tps = (rows // sc2.shape[0]) // tm if per_seq else None
    ms = _mod_spec(per_seq, tm, tps)
    width = w_router_pad.shape[1]
    return pl.pallas_call(
        _router_body,
        grid=(rows // tm,),
        in_specs=[pl.BlockSpec((tm, d), lambda i: (i, 0)), ms, ms,
                  pl.BlockSpec((1, d), lambda i: (0, 0)),
                  pl.BlockSpec((d, width), lambda i: (0, 0))],
        out_specs=[pl.BlockSpec((tm, width), lambda i: (i, 0)), pl.BlockSpec((1, width), lambda i: (0, 0))],
        out_shape=[jax.ShapeDtypeStruct((rows, width), F32), jax.ShapeDtypeStruct((1, width), F32)],
        scratch_shapes=[pltpu.VMEM((1, width), F32)],
        compiler_params=_cparams("arbitrary"),
        name="router",
    )(x, sc2, sh2, norm_ffn, w_router_pad)


def _route_plan(route, counts, tm):
    rows = route.shape[0]
    n_tiles = -(-2 * rows // tm) + N_EXPERTS + 1
    i1 = route[:, R_I1].astype(jnp.int32)
    i2 = route[:, R_I2].astype(jnp.int32)
    cnt = counts[0, :N_EXPERTS].astype(jnp.int32)
    tiles = (cnt + tm - 1) // tm
    ends = jnp.cumsum(tiles)
    starts = (ends - tiles) * tm
    pos1 = starts[i1] + route[:, R_R1].astype(jnp.int32)
    pos2 = starts[i2] + route[:, R_R2].astype(jnp.int32)
    tile_expert = jnp.minimum(jnp.sum(jnp.arange(n_tiles)[:, None] >= ends[None, :], axis=1), N_EXPERTS - 1)
    tok = jnp.arange(rows, dtype=jnp.int32)
    row_token = jnp.zeros((n_tiles * tm,), jnp.int32).at[pos1].set(tok).at[pos2].set(tok)
    return dict(n_tiles=n_tiles, tile_expert=tile_expert.astype(jnp.int32), n_used=ends[-1:].astype(jnp.int32),
                row_token=row_token.reshape(n_tiles, 1, tm), pos1=pos1, pos2=pos2)


def _start_row_gather(idx_ref, n, src_hbm, dst, sem):
    def body(r, carry):
        src = pl.multiple_of(idx_ref[0, 0, r] * ROW_SLABS, ROW_SLABS)
        pltpu.make_async_copy(src_hbm.at[pl.ds(src, ROW_SLABS), :],
                              dst.at[pl.ds(pl.multiple_of(r * ROW_SLABS, ROW_SLABS), ROW_SLABS), :], sem).start()
        return carry
    lax.fori_loop(0, n, body, 0, unroll=8)


def _wait_row_gather(n, src_hbm, dst, sem):
    pltpu.make_async_copy(src_hbm.at[pl.ds(0, n * ROW_SLABS), :], dst, sem).wait()


def _moe_ffn_body(te_ref, nu_ref, rt_ref, rtn_ref, h_hbm, wg_ref, wu_ref, wd_ref, o_ref,
                  xbuf, hbuf, acc, sem, *, tm, nj):
    i = pl.program_id(0)
    j = pl.program_id(1)
    used = i < nu_ref[0]
    slot = i % 2
    per_step = tm // nj

    @pl.when(jnp.logical_and(j == 0, i == 0))
    def _():
        _start_row_gather(rt_ref, tm, h_hbm, xbuf.at[0], sem.at[0])

    @pl.when(jnp.logical_and(j == 0, i <= nu_ref[0]))
    def _():
        _wait_row_gather(tm, h_hbm, xbuf.at[slot], sem.at[slot])

    @pl.when(jnp.logical_and(j == 0, used))
    def _():
        for s in range(ROW_SLABS):
            hbuf[:, s * LANES:(s + 1) * LANES] = xbuf[slot, pl.ds(s, tm, stride=ROW_SLABS), :].astype(BF16)
        acc[...] = jnp.zeros_like(acc)

    @pl.when(used)
    def _():
        for k in range(per_step):
            r = j * per_step + k
            src = pl.multiple_of(rtn_ref[0, 0, r] * ROW_SLABS, ROW_SLABS)
            pltpu.make_async_copy(
                h_hbm.at[pl.ds(src, ROW_SLABS), :],
                xbuf.at[1 - slot, pl.ds(pl.multiple_of(r * ROW_SLABS, ROW_SLABS), ROW_SLABS), :],
                sem.at[1 - slot]).start()
        _swiglu_accumulate(hbuf[...], wg_ref, wu_ref, wd_ref, acc)

    last = j == pl.num_programs(1) - 1

    @pl.when(jnp.logical_and(last, used))
    def _():
        for s in range(ROW_SLABS):
            o_ref[pl.ds(s, tm, stride=ROW_SLABS), :] = acc[:, s * LANES:(s + 1) * LANES]

    @pl.when(jnp.logical_and(last, jnp.logical_not(used)))
    def _():
        o_ref[...] = jnp.zeros_like(o_ref)


def _moe_ffn(h_rows, plan, w_up, w_down, moe_idx, *, tm, tf):
    d = w_up.shape[-2]
    ff = w_down.shape[-2]
    nj = ff // tf
    n_tiles = plan["n_tiles"]

    def jj(i, j, nu):
        return jnp.where(i < nu[0], j, nj - 1)

    grid_spec = pltpu.PrefetchScalarGridSpec(
        num_scalar_prefetch=2,
        grid=(n_tiles, nj),
        in_specs=[pl.BlockSpec((1, 1, tm), lambda i, j, te, nu: (i, 0, 0), memory_space=pltpu.SMEM),
                  pl.BlockSpec((1, 1, tm), lambda i, j, te, nu: (jnp.minimum(i + 1, n_tiles - 1), 0, 0),
                               memory_space=pltpu.SMEM),
                  pl.BlockSpec(memory_space=pl.ANY),
                  pl.BlockSpec((None, None, d, tf), lambda i, j, te, nu: (moe_idx, te[i], 0, jj(i, j, nu))),
                  pl.BlockSpec((None, None, d, tf), lambda i, j, te, nu: (moe_idx, te[i], 0, jj(i, j, nu) + nj)),
                  pl.BlockSpec((None, None, tf, d), lambda i, j, te, nu: (moe_idx, te[i], jj(i, j, nu), 0))],
        out_specs=pl.BlockSpec((tm * ROW_SLABS, LANES), lambda i, j, te, nu: (i, 0)),
        scratch_shapes=[pltpu.VMEM((2, tm * ROW_SLABS, LANES), F32), pltpu.VMEM((tm, d), BF16),
                        pltpu.VMEM((tm, d), F32), pltpu.SemaphoreType.DMA((2,))])
    return pl.pallas_call(
        functools.partial(_moe_ffn_body, tm=tm, nj=nj),
        grid_spec=grid_spec,
        out_shape=jax.ShapeDtypeStruct((n_tiles * tm * ROW_SLABS, LANES), F32),
        compiler_params=pltpu.CompilerParams(dimension_semantics=("arbitrary", "arbitrary"),
                                             vmem_limit_bytes=VMEM_LIMIT, disable_bounds_checks=True),
        name="moe_ffn",
    )(plan["tile_expert"], plan["n_used"], plan["row_token"], plan["row_token"], h_rows, w_up, w_up, w_down)


def _moe_combine_body(p1_ref, p2_ref, p1n_ref, p2n_ref, y_hbm, x_ref, g2_ref, route_ref, fw_ref, o_ref,
                      buf1, buf2, sem, *, tc, final):
    i = pl.program_id(0)
    slot = i % 2

    @pl.when(i == 0)
    def _():
        _start_row_gather(p1_ref, tc, y_hbm, buf1.at[0], sem.at[0, 0])
        _start_row_gather(p2_ref, tc, y_hbm, buf2.at[0], sem.at[1, 0])

    _wait_row_gather(tc, y_hbm, buf1.at[slot], sem.at[0, slot])
    _wait_row_gather(tc, y_hbm, buf2.at[slot], sem.at[1, slot])

    @pl.when(i + 1 < pl.num_programs(0))
    def _():
        _start_row_gather(p1n_ref, tc, y_hbm, buf1.at[1 - slot], sem.at[0, 1 - slot])
        _start_row_gather(p2n_ref, tc, y_hbm, buf2.at[1 - slot], sem.at[1, 1 - slot])

    w1 = route_ref[:, R_W1:R_W1 + 1]
    w2 = route_ref[:, R_W2:R_W2 + 1]
    pieces = []
    for s in range(ROW_SLABS):
        cols = slice(s * LANES, (s + 1) * LANES)
        y = (w1 * buf1[slot, pl.ds(s, tc, stride=ROW_SLABS), :]
             + w2 * buf2[slot, pl.ds(s, tc, stride=ROW_SLABS), :])
        pieces.append(x_ref[:, cols] + g2_ref[:, cols] * y)
    xn = jnp.concatenate(pieces, axis=1)
    if final:
        xn = xn * lax.rsqrt(jnp.mean(xn * xn, axis=-1, keepdims=True) + NORM_EPS) * fw_ref[...]
    o_ref[...] = xn


def _moe_combine(y_rows, plan, route, x, g2, final_w, *, tc, per_seq, final):
    rows, d = x.shape
    nt = rows // tc
    tps = (rows // g2.shape[0]) // tc if per_seq else None
    p1 = plan["pos1"].reshape(nt, 1, tc)
    p2 = plan["pos2"].reshape(nt, 1, tc)
    cur = pl.BlockSpec((1, 1, tc), lambda i: (i, 0, 0), memory_space=pltpu.SMEM)
    nxt = pl.BlockSpec((1, 1, tc), lambda i: (jnp.minimum(i + 1, nt - 1), 0, 0), memory_space=pltpu.SMEM)
    return pl.pallas_call(
        functools.partial(_moe_combine_body, tc=tc, final=final),
        grid=(nt,),
        in_specs=[cur, cur, nxt, nxt, pl.BlockSpec(memory_space=pl.ANY),
                  pl.BlockSpec((tc, d), lambda i: (i, 0)), _mod_spec(per_seq, tc, tps),
                  pl.BlockSpec((tc, route.shape[1]), lambda i: (i, 0)),
                  pl.BlockSpec((1, d), lambda i: (0, 0))],
        out_specs=pl.BlockSpec((tc, d), lambda i: (i, 0)),
        out_shape=jax.ShapeDtypeStruct((rows, d), F32),
        scratch_shapes=[pltpu.VMEM((2, tc * ROW_SLABS, LANES), F32), pltpu.VMEM((2, tc * ROW_SLABS, LANES), F32),
                        pltpu.SemaphoreType.DMA((2, 2))],
        compiler_params=pltpu.CompilerParams(dimension_semantics=("arbitrary",),
                                             vmem_limit_bytes=VMEM_LIMIT, disable_bounds_checks=True),
        name="moe_combine",
    )(p1, p2, p1, p2, y_rows, x, g2, route, final_w)


def _final_body(x_ref, w_ref, o_ref):
    x = x_ref[...]
    o_ref[...] = x * lax.rsqrt(jnp.mean(x * x, axis=-1, keepdims=True) + NORM_EPS) * w_ref[...]


def _final_norm(x, w, *, tm):
    rows, d = x.shape
    return pl.pallas_call(
        _final_body,
        grid=(rows // tm,),
        in_specs=[pl.BlockSpec((tm, d), lambda i: (i, 0)), pl.BlockSpec((1, d), lambda i: (0, 0))],
        out_specs=pl.BlockSpec((tm, d), lambda i: (i, 0)),
        out_shape=jax.ShapeDtypeStruct((rows, d), F32),
        compiler_params=_cparams("arbitrary"),
        name="final_norm",
    )(x, w)


def _rope_tables(pos):
    half = A_HEAD_DIM // 2
    inv = jnp.exp(-math.log(ROPE_THETA) * jnp.arange(half, dtype=F32) / half)
    ang = pos.astype(F32)[:, None] * inv[None, :]
    cos, sin = jnp.cos(ang), jnp.sin(ang)
    cos_t = jnp.tile(jnp.concatenate([cos, cos], axis=1), (1, HEADS_PER_SLAB))
    sin_t = jnp.tile(jnp.concatenate([-sin, sin], axis=1), (1, HEADS_PER_SLAB))
    return cos_t, sin_t


def _slab_rows(pa, c, nseq, seq_len, keep):
    if keep == seq_len:
        t = pa[c * SLABS:(c + 1) * SLABS]
    else:
        t = jnp.stack([lax.slice(pa, (c * SLABS, n * seq_len + seq_len - keep, 0),
                                 ((c + 1) * SLABS, (n + 1) * seq_len, LANES)) for n in range(nseq)], axis=1)
    t = t.reshape(SLABS, nseq, keep, HEADS_PER_SLAB, A_HEAD_DIM)
    return t.transpose(1, 2, 0, 3, 4).reshape(nseq, keep, A_HEADS, A_HEAD_DIM)


def _split_w_in(w):
    o_b = A_QKV
    o_z = o_b + CONV_CH
    o_ba = o_z + B_WIDTH
    o_g = o_ba + 2 * B_HEADS
    pad = jnp.zeros((w.shape[0], BA_PAD - 2 * B_HEADS), w.dtype)
    main = jnp.concatenate([w[:, o_b:o_ba], w[:, o_g:], w[:, o_ba:o_g], pad], axis=1)
    return w[:, :A_QKV].astype(BF16), main.astype(BF16)


def kernel(x_prompt, x_sample, cache_k_g0, cache_v_g0, cache_k_g1, cache_v_g1, cache_k_g2, cache_v_g2,
           state_conv, state_delta, c_prompt, c_sample, w_mod, b_mod, norm_mix, norm_ffn, w_in, conv_w,
           a_log, dt_bias, gdn_norm, w_branch_a, w_branch_b, w_out, w_ffn_up, w_ffn_down, w_router,
           w_exp_up, w_exp_down, final_norm):
    depth = w_in.shape[0]
    nb, seq, d = x_prompt.shape
    ns, dec, _ = x_sample.shape
    past = cache_k_g2.shape[2]

    c_all = jnp.concatenate([c_prompt, c_sample, jnp.zeros((-(nb + ns) % 8, d), F32)], axis=0)
    mod_all = _modulation(c_all, w_mod, b_mod)

    caches = [_window_minor(c) for c in (cache_k_g0, cache_v_g0, cache_k_g1, cache_v_g1, cache_k_g2, cache_v_g2)]
    tabs_p = _rope_tables(jnp.tile(jnp.arange(seq), nb))
    tabs_s = _rope_tables(jnp.tile(past + jnp.arange(dec), ns))
    zero_conv = jnp.zeros((depth, nb, CONV_W - 1, CONV_CH), F32)
    zero_state = jnp.zeros((depth, nb, B_HEADS, B_HEAD_DIM, B_HEAD_DIM), F32)

    exp_up = w_exp_up.astype(BF16)
    exp_down = w_exp_down.astype(BF16)
    xp = x_prompt.reshape(nb * seq, d)
    xs = x_sample.reshape(ns * dec, d)
    new_p = [[] for _ in range(8)]
    new_s = [[] for _ in range(8)]
    tm_p, tm_s = 512, 512

    for l in range(depth):
        mods_p = [m.reshape(nb, 1, d) for m in jnp.split(mod_all[l, :nb], 6, axis=-1)]
        mods_s = [jnp.repeat(m, dec, axis=0) for m in jnp.split(mod_all[l, nb:nb + ns], 6, axis=-1)]
        w_attn, w_main = _split_w_in(w_in[l])
        nw_mix = norm_mix[l].reshape(1, d)
        nw_ffn = norm_ffn[l].reshape(1, d)
        a_par = jnp.zeros((2, BA_PAD), F32)
        a_par = a_par.at[0, B_HEADS:2 * B_HEADS].set(a_log[l]).at[1, B_HEADS:2 * B_HEADS].set(dt_bias[l])
        wa = w_branch_a[l].astype(BF16)
        wb = w_branch_b[l].astype(BF16)
        wo = w_out[l].astype(BF16)
        gn = gdn_norm[l].reshape(1, B_HEAD_DIM)

        outs = []
        for (x, mods, per_seq, tm, tabs, is_prompt) in ((xp, mods_p, True, tm_p, tabs_p, True),
                                                        (xs, mods_s, False, tm_s, tabs_s, False)):
            sh1, sc1, g1, sh2, sc2, g2 = mods
            tm_proj = min(x.shape[0], PROJ_ROWS)
            pm = _project(x, sc1, sh1, nw_mix, w_main, tm=tm_proj, tn=MAIN_COLS // 5, per_seq=per_seq,
                          name="proj_main")
            pa = _project(x, sc1, sh1, nw_mix, w_attn, tm=tm_proj, tn=3 * A_WIDTH, per_seq=per_seq,
                          rope_tabs=tabs, name="proj_attn")
            if is_prompt:
                run = None
                for gi in range(N_GROUPS):
                    run = _band_attention(pa, gi, seq, run)
                o_a = run[0]
                o_b, s_new = _delta_rule(pm, zero_conv, zero_state, l, conv_w[l], a_par, seq, PROMPT_CHUNK, nb)
            else:
                o_a = _cached_attention(pa, caches, l, dec)
                o_b, s_new = _delta_rule(pm, state_conv, state_delta, l, conv_w[l], a_par, dec, dec, SAMPLE_SEQS_PER_STEP)
            x, h2 = _mixer_out(x, o_a, o_b, pm, g1, sc2, sh2, gn, nw_ffn, wa, wb, wo, tm=tm, per_seq=per_seq,
                               token_major=l % 2 == 1)
            if l % 2 == 0:
                x = _ffn(h2, x, g2, w_ffn_up[l // 2].astype(BF16), w_ffn_down[l // 2].astype(BF16),
                         tm=tm, tf=1408, per_seq=per_seq)
            else:
                wr = jnp.zeros((d, LANES), F32).at[:, :N_EXPERTS].set(w_router[l // 2])
                route, counts = _router(x, sc2, sh2, nw_ffn, wr, tm=tm, per_seq=per_seq)
                plan = _route_plan(route, counts, 512 if is_prompt else 256)
                y_rows = _moe_ffn(h2, plan, exp_up, exp_down, l // 2, tm=512 if is_prompt else 256, tf=1792)
                x = _moe_combine(y_rows, plan, route, x, g2, final_norm.reshape(1, d), tc=256,
                                 per_seq=per_seq, final=l == depth - 1)
            outs.append((x, pa, pm, s_new))

        (xp, pa_p, pm_p, s_p), (xs, pa_s, pm_s, s_s) = outs
        for gi, (win, dil) in enumerate(ATTN_GROUPS):
            keep = min(win, seq)
            new_p[2 * gi].append(_slab_rows(pa_p, 3 * gi + 1, nb, seq, keep))
            new_p[2 * gi + 1].append(_slab_rows(pa_p, 3 * gi + 2, nb, seq, keep))
            new_s[2 * gi].append(_slab_rows(pa_s, 3 * gi + 1, ns, dec, dec))
            new_s[2 * gi + 1].append(_slab_rows(pa_s, 3 * gi + 2, ns, dec, dec))
        new_p[6].append(pm_p.reshape(nb, seq, MAIN_COLS)[:, seq - (CONV_W - 1):, :CONV_CH])
        new_p[7].append(s_p)
        conv_s = jnp.concatenate([state_conv[l], pm_s.reshape(ns, dec, MAIN_COLS)[:, :, :CONV_CH]], axis=1)
        new_s[6].append(conv_s[:, -(CONV_W - 1):])
        new_s[7].append(s_s)

    if depth % 2 == 1:
        xp = _final_norm(xp, final_norm.reshape(1, d), tm=tm_p)
        xs = _final_norm(xs, final_norm.reshape(1, d), tm=tm_s)
    y_prompt = xp.reshape(nb, seq, d)
    y_sample = xs.reshape(ns, dec, d)
    return (y_prompt, y_sample, *[jnp.stack(t) for t in new_p], *[jnp.stack(t) for t in new_s])
```

```python
import functools
import math

import numpy as np
import jax
import jax.numpy as jnp
from jax import lax
from jax.experimental import pallas as pl
from jax.experimental.pallas import tpu as pltpu

F32 = jnp.float32
BF16 = jnp.bfloat16

D_MODEL = 1024
ATTN_GROUPS = ((128, 1), (512, 4), (2048, 16))
N_GROUPS = 3
A_HEADS = 4
A_HEAD_DIM = 64
A_WIDTH = A_HEADS * A_HEAD_DIM
LANES = 128
SLABS = A_WIDTH // LANES
HEADS_PER_SLAB = LANES // A_HEAD_DIM
ROW_SLABS = D_MODEL // LANES
ROPE_THETA = 10000.0
BAND_BLOCK = 128
BAND_TILING = ((8, 1), (4, 1), (1, 2))
B_HEADS = 8
B_HEAD_DIM = 128
B_WIDTH = B_HEADS * B_HEAD_DIM
CONV_W = 4
CONV_CH = 3 * B_WIDTH
PROMPT_CHUNK = 128
SAMPLE_SEQS_PER_STEP = 4
PROJ_ROWS = 1024
MOE_ROWS = 512
N_EXPERTS = 8
NORM_EPS = 1e-6
A_QKV = N_GROUPS * 3 * A_WIDTH
NEG = -1e30

BA_PAD = 256
MAIN_COLS = CONV_CH + 3 * D_MODEL + BA_PAD
COL_Z = CONV_CH // D_MODEL
COL_GA = COL_Z + 1
COL_GB = COL_Z + 2
COL_BA = (CONV_CH + 3 * D_MODEL) // BA_PAD

VMEM_LIMIT = 52 * 1024 * 1024


def _cparams(*sem):
    return pltpu.CompilerParams(dimension_semantics=sem, vmem_limit_bytes=VMEM_LIMIT)


def _sigmoid(x):
    return 1.0 / (1.0 + jnp.exp(-x))


def _silu(x):
    return x * _sigmoid(x)


def _nt_dot(a, b):
    return lax.dot_general(a, b, (((1,), (1,)), ((), ())), preferred_element_type=F32)


def _tn_dot(a, b):
    return lax.dot_general(a, b, (((0,), (0,)), ((), ())), preferred_element_type=F32)


def _mod_spec(per_seq, tm, tiles_per_seq):
    if per_seq:
        return pl.BlockSpec((None, 1, D_MODEL), lambda i, *_: (i // tiles_per_seq, 0, 0))
    return pl.BlockSpec((tm, D_MODEL), lambda i, *_: (i, 0))


def _rms_mod(x, nw, sc, sh):
    y = x * lax.rsqrt(jnp.mean(x * x, axis=-1, keepdims=True) + NORM_EPS) * nw
    return y * (1.0 + sc) + sh


def _mod_body(c_ref, w_ref, b_ref, o_ref):
    s = _silu(c_ref[...]).astype(BF16)
    o_ref[...] = jnp.dot(s, w_ref[...].astype(BF16), preferred_element_type=F32) + b_ref[...]


def _modulation(c_all, w_mod, b_mod):
    depth, d, n = w_mod.shape
    rows = c_all.shape[0]
    tn = 1536
    return pl.pallas_call(
        _mod_body,
        grid=(depth, n // tn),
        in_specs=[pl.BlockSpec((rows, d), lambda l, j: (0, 0)),
                  pl.BlockSpec((None, d, tn), lambda l, j: (l, 0, j)),
                  pl.BlockSpec((None, 1, tn), lambda l, j: (l, 0, j))],
        out_specs=pl.BlockSpec((None, rows, tn), lambda l, j: (l, 0, j)),
        out_shape=jax.ShapeDtypeStruct((depth, rows, n), F32),
        compiler_params=_cparams("arbitrary", "arbitrary"),
        name="adaln_mod",
    )(c_all, w_mod, b_mod.reshape(depth, 1, n))


def _proj_body(x_ref, sc_ref, sh_ref, nw_ref, w_ref, *rest, rope):
    if rope:
        cos_ref, sin_ref, o_ref, h_scr = rest
    else:
        o_ref, h_scr = rest

    @pl.when(pl.program_id(1) == 0)
    def _():
        h_scr[...] = _rms_mod(x_ref[...], nw_ref[...], sc_ref[...], sh_ref[...]).astype(BF16)

    y = jnp.dot(h_scr[...], w_ref[...], preferred_element_type=F32)
    if not rope:
        o_ref[...] = y
        return
    cos = cos_ref[...]
    sin = sin_ref[...]
    lane = lax.broadcasted_iota(jnp.int32, cos.shape, 1)
    first_half = (lane % A_HEAD_DIM) < (A_HEAD_DIM // 2)
    for c in range(3 * SLABS):
        t = y[:, c * LANES:(c + 1) * LANES]
        if c < 2 * SLABS:
            partner = jnp.where(first_half, pltpu.roll(t, LANES - A_HEAD_DIM // 2, 1),
                                pltpu.roll(t, A_HEAD_DIM // 2, 1))
            t = t * cos + partner * sin
        o_ref[c] = t


def _project(x, sc, sh, nw, w, *, tm, tn, per_seq, rope_tabs=None, name):
    rows, d = x.shape
    n = w.shape[1]
    tps = (rows // sc.shape[0]) // tm if per_seq else None
    in_specs = [pl.BlockSpec((tm, d), lambda i, j: (i, 0)),
                _mod_spec(per_seq, tm, tps), _mod_spec(per_seq, tm, tps),
                pl.BlockSpec((1, d), lambda i, j: (0, 0)),
                pl.BlockSpec((d, tn), lambda i, j: (0, j))]
    args = [x, sc, sh, nw, w]
    out_spec = pl.BlockSpec((tm, tn), lambda i, j: (i, j))
    out_shape = jax.ShapeDtypeStruct((rows, n), F32)
    if rope_tabs is not None:
        in_specs += [pl.BlockSpec((tm, LANES), lambda i, j: (i, 0))] * 2
        args += list(rope_tabs)
        out_spec = pl.BlockSpec((tn // LANES, tm, LANES), lambda i, j: (j, i, 0))
        out_shape = jax.ShapeDtypeStruct((n // LANES, rows, LANES), F32)
    return pl.pallas_call(
        functools.partial(_proj_body, rope=rope_tabs is not None),
        grid=(rows // tm, n // tn),
        in_specs=in_specs,
        out_specs=out_spec,
        out_shape=out_shape,
        scratch_shapes=[pltpu.VMEM((tm, d), BF16)],
        compiler_params=_cparams("arbitrary", "arbitrary"),
        name=name,
    )(*args)


def _head_masks(width):
    lane = lax.broadcasted_iota(jnp.int32, (1, width), 1)
    return [(lane // A_HEAD_DIM == h).astype(F32) for h in range(A_HEADS)]


def _merge_lse(o_a, l_a, o_b, l_b):
    m = jnp.maximum(l_a, l_b)
    e_a = jnp.exp(l_a - m)
    e_b = jnp.exp(l_b - m)
    den = e_a + e_b
    return (o_a * e_a + o_b * e_b) / den, m + jnp.log(den)


def _band_body(q_ref, kc_ref, kp_ref, vc_ref, vp_ref, bias_ref, *rest, dil, nb, res_per_iter, has_prev_group):
    if has_prev_group:
        oprev_ref, lprev_ref, o_ref, l_ref = rest
    else:
        o_ref, l_ref = rest
    blk = BAND_BLOCK
    col = lax.broadcasted_iota(jnp.int32, (blk, 2 * blk), 1)
    no_prev = jnp.logical_and(pl.program_id(1) == 0, col < blk)
    bias_h = jnp.concatenate([bias_ref[...]] * HEADS_PER_SLAB, axis=0)
    bias_h0 = jnp.concatenate([jnp.where(no_prev, NEG, bias_ref[...])] * HEADS_PER_SLAB, axis=0)
    hm = _head_masks(LANES)

    def rows_of(r, b):
        return pl.ds(r + dil * blk * b, blk, stride=dil) if dil > 1 else pl.ds(blk * b, blk)

    def with_prev(cur_ref, prev_ref, sl, r, b):
        prev = prev_ref[sl, rows_of(r, 0), :] if b == 0 else cur_ref[sl, rows_of(r, b - 1), :]
        return jnp.concatenate([prev, cur_ref[sl, rows_of(r, b), :]], axis=0).astype(BF16)

    def process(residues):
        units = [dict(r=r, b=b, sl=sl) for r in residues for b in range(nb) for sl in range(SLABS)]
        for un in units:
            q = q_ref[un["sl"], rows_of(un["r"], un["b"]), :] * (A_HEAD_DIM ** -0.5)
            qh = jnp.concatenate([q * hm[h] for h in range(HEADS_PER_SLAB)], axis=0).astype(BF16)
            k = with_prev(kc_ref, kp_ref, un["sl"], un["r"], un["b"])
            un["s"] = _nt_dot(qh, k) + (bias_h0 if un["b"] == 0 else bias_h)
        for un in units:
            m = jnp.max(un["s"], axis=1, keepdims=True)
            p = jnp.exp(un["s"] - m)
            un["l"] = jnp.sum(p, axis=1, keepdims=True)
            un["m"] = m
            un["p"] = p.astype(BF16)
        for un in units:
            v = with_prev(vc_ref, vp_ref, un["sl"], un["r"], un["b"])
            un["oh"] = jnp.dot(un["p"], v, preferred_element_type=F32)
        for un in units:
            rows = rows_of(un["r"], un["b"])
            oh = un["oh"] / un["l"]
            lseh = un["m"] + jnp.log(un["l"])
            o = sum(oh[h * blk:(h + 1) * blk] * hm[h] for h in range(HEADS_PER_SLAB))
            lse = sum(lseh[h * blk:(h + 1) * blk] * hm[h] for h in range(HEADS_PER_SLAB))
            if has_prev_group:
                o, lse = _merge_lse(oprev_ref[un["sl"], rows, :], lprev_ref[un["sl"], rows, :], o, lse)
            o_ref[un["sl"], rows, :] = o
            l_ref[un["sl"], rows, :] = lse

    if dil <= res_per_iter:
        process(list(range(dil)))
    else:
        def loop_body(it, carry):
            process([it * res_per_iter + k for k in range(res_per_iter)])
            return carry
        lax.fori_loop(0, dil // res_per_iter, loop_body, 0)


def _band_bias(nk):
    qi = np.arange(BAND_BLOCK)[:, None]
    kj = np.arange(2 * BAND_BLOCK)[None, :]
    rel = kj - BAND_BLOCK - qi
    return jnp.asarray(np.where((rel <= 0) & (rel >= -nk), 0.0, NEG), F32)


def _band_attention(qkv, gi, seq_len, prev):
    win, dil = ATTN_GROUPS[gi]
    rows = qkv.shape[1]
    nb, res_per_iter = BAND_TILING[gi]
    blk_rows = BAND_BLOCK * dil
    nb = min(nb, seq_len // blk_rows)
    sb_rows = blk_rows * nb
    nsb = seq_len // sb_rows
    nseq = rows // seq_len

    def cur(c):
        return pl.BlockSpec((SLABS, sb_rows, LANES), lambda n, s: (c, n * nsb + s, 0))

    def prv(c):
        return pl.BlockSpec((SLABS, blk_rows, LANES),
                            lambda n, s: (c, jnp.maximum((n * nsb + s) * nb - 1, n * nsb * nb), 0))

    row_spec = pl.BlockSpec((SLABS, sb_rows, LANES), lambda n, s: (0, n * nsb + s, 0))
    in_specs = [cur(3 * gi), cur(3 * gi + 1), prv(3 * gi + 1), cur(3 * gi + 2), prv(3 * gi + 2),
                pl.BlockSpec((BAND_BLOCK, 2 * BAND_BLOCK), lambda n, s: (0, 0))]
    args = [qkv, qkv, qkv, qkv, qkv, _band_bias(win // dil)]
    if prev is not None:
        in_specs += [row_spec, row_spec]
        args += list(prev)
    return pl.pallas_call(
        functools.partial(_band_body, dil=dil, nb=nb, res_per_iter=res_per_iter, has_prev_group=prev is not None),
        grid=(nseq, nsb),
        in_specs=in_specs,
        out_specs=[row_spec, row_spec],
        out_shape=[jax.ShapeDtypeStruct((SLABS, rows, LANES), F32)] * 2,
        compiler_params=_cparams("arbitrary", "arbitrary"),
        name="band_attn_g%d" % gi,
    )(*args)


def _cached_body(row_ref, *rest, n_new):
    caches = rest[:2 * N_GROUPS]
    biases = rest[2 * N_GROUPS:4 * N_GROUPS]
    o_ref = rest[4 * N_GROUPS]
    hm = _head_masks(LANES)
    units = [dict(sl=sl, gi=gi) for sl in range(SLABS) for gi in range(N_GROUPS)]
    for un in units:
        sl, gi = un["sl"], un["gi"]
        q = row_ref[(3 * gi) * SLABS + sl] * (A_HEAD_DIM ** -0.5)
        kn = row_ref[(3 * gi + 1) * SLABS + sl].astype(BF16)
        qh = jnp.concatenate([q * hm[h] for h in range(HEADS_PER_SLAB)], axis=0).astype(BF16)
        kct = caches[2 * gi][sl].astype(BF16)
        un["s_c"] = jnp.dot(qh, kct, preferred_element_type=F32) + biases[2 * gi][...]
        un["s_n"] = _nt_dot(qh, kn) + biases[2 * gi + 1][...]
    for un in units:
        m = jnp.maximum(jnp.max(un["s_c"], axis=1, keepdims=True), jnp.max(un["s_n"], axis=1, keepdims=True))
        p_c = jnp.exp(un["s_c"] - m)
        p_n = jnp.exp(un["s_n"] - m)
        un["l"] = jnp.sum(p_c, axis=1, keepdims=True) + jnp.sum(p_n, axis=1, keepdims=True)
        un["m"] = m
        un["p_c"] = p_c.astype(BF16)
        un["p_n"] = p_n.astype(BF16)
    for un in units:
        sl, gi = un["sl"], un["gi"]
        vn = row_ref[(3 * gi + 2) * SLABS + sl].astype(BF16)
        vct = caches[2 * gi + 1][sl].astype(BF16)
        un["oh"] = _nt_dot(un["p_c"], vct) + jnp.dot(un["p_n"], vn, preferred_element_type=F32)
    for sl in range(SLABS):
        o_run = l_run = None
        for un in units[sl * N_GROUPS:(sl + 1) * N_GROUPS]:
            oh = un["oh"] / un["l"]
            lseh = un["m"] + jnp.log(un["l"])
            o = sum(oh[h * n_new:(h + 1) * n_new] * hm[h] for h in range(HEADS_PER_SLAB))
            lse = sum(lseh[h * n_new:(h + 1) * n_new] * hm[h] for h in range(HEADS_PER_SLAB))
            if un["gi"] == 0:
                o_run, l_run = o, lse
            else:
                o_run, l_run = _merge_lse(o_run, l_run, o, lse)
        o_ref[sl] = o_run


def _cached_bias(wb, n_new, win, dil):
    nk = win // dil
    t = np.tile(np.arange(n_new), HEADS_PER_SLAB)[:, None]
    delta = wb + t - np.arange(wb + n_new)[None, :]
    ok = (delta >= 0) & (delta % dil == 0) & (delta <= dil * nk)
    b = np.where(ok, 0.0, NEG).astype(np.float32)
    return jnp.asarray(b[:, :wb]), jnp.asarray(b[:, wb:])


def _window_minor(c):
    depth, n, wb = c.shape[:3]
    return c.transpose(0, 1, 3, 4, 2).reshape(depth, n, SLABS, LANES, wb)


def _cached_attention(qkv, caches, layer, n_new):
    nslab, rows, _ = qkv.shape
    nseq = rows // n_new
    in_specs = [pl.BlockSpec((nslab, n_new, LANES), lambda n: (0, n, 0))]
    args = [qkv]
    for c in caches:
        in_specs.append(pl.BlockSpec((None, None, SLABS, LANES, c.shape[4]), lambda n: (layer, n, 0, 0, 0)))
        args.append(c)
    for gi, (win, dil) in enumerate(ATTN_GROUPS):
        for b in _cached_bias(caches[2 * gi].shape[4], n_new, win, dil):
            in_specs.append(pl.BlockSpec(b.shape, lambda n: (0, 0)))
            args.append(b)
    return pl.pallas_call(
        functools.partial(_cached_body, n_new=n_new),
        grid=(nseq,),
        in_specs=in_specs,
        out_specs=pl.BlockSpec((SLABS, n_new, LANES), lambda n: (0, n, 0)),
        out_shape=jax.ShapeDtypeStruct((SLABS, rows, LANES), F32),
        compiler_params=_cparams("arbitrary"),
        name="cached_attn",
    )(*args)


def _cumsum_rows(x):
    n = x.shape[0]
    row = lax.broadcasted_iota(jnp.int32, x.shape, 0)
    sh = 1
    while sh < n:
        x = x + jnp.where(row >= sh, pltpu.roll(x, sh, 0), 0.0)
        sh *= 2
    return x


def _delta_body(u_ref, ba_ref, conv0_ref, s0_ref, cw_ref, ap_ref, o_ref, s_ref, ubuf, *, chunk, nsb):
    C = chunk
    f32dot = functools.partial(jnp.dot, preferred_element_type=F32)

    @pl.when(pl.program_id(1) == 0)
    def _():
        ubuf[:, 0:8, :] = jnp.zeros((nsb, 8, CONV_CH), F32)
        ubuf[:, 8 - (CONV_W - 1):8, :] = conv0_ref[...]
        s_ref[...] = s0_ref[...]

    ubuf[:, 8:8 + C, :] = u_ref[...]
    cw = cw_ref[...]
    ap = ap_ref[...]
    ri = lax.broadcasted_iota(jnp.int32, (C, C), 0)
    ci = lax.broadcasted_iota(jnp.int32, (C, C), 1)
    eye_f = (ri == ci).astype(F32)
    levels = []
    b = 1
    while b < C:
        levels.append(((ri // (2 * b)) == (ci // (2 * b))) & ((ri // b) % 2 == 1) & ((ci // b) % 2 == 0))
        b *= 2

    units = []
    for sq in range(nsb):
        conv = ubuf[sq, 8:8 + C, :] * cw[CONV_W - 1:CONV_W]
        for i in range(CONV_W - 1):
            off = 8 - (CONV_W - 1) + i
            conv = conv + ubuf[sq, off:off + C, :] * cw[i:i + 1]
        cu = _silu(conv)
        ba = ba_ref[sq]
        beta_f = _sigmoid(ba)
        z = ba + ap[1:2]
        softplus = jnp.maximum(z, 0.0) + jnp.log(1.0 + jnp.exp(-jnp.abs(z)))
        g_cum = _cumsum_rows(-jnp.exp(ap[0:1]) * softplus)
        g_rows = g_cum[:, 0:LANES].T
        for h in range(B_HEADS):
            lo = h * B_HEAD_DIM
            qh = cu[:, lo:lo + B_HEAD_DIM]
            kh = cu[:, B_WIDTH + lo:B_WIDTH + lo + B_HEAD_DIM]
            qn = qh * lax.rsqrt(jnp.sum(qh * qh, axis=-1, keepdims=True) + NORM_EPS) * (B_HEAD_DIM ** -0.5)
            kn = kh * lax.rsqrt(jnp.sum(kh * kh, axis=-1, keepdims=True) + NORM_EPS)
            g_col = g_cum[:, B_HEADS + h:B_HEADS + h + 1]
            units.append(dict(sq=sq, h=h, qn=qn, kn=kn, kb=kn.astype(BF16),
                              vh=cu[:, 2 * B_WIDTH + lo:2 * B_WIDTH + lo + B_HEAD_DIM],
                              beta=beta_f[:, h:h + 1], g_col=g_col,
                              g_row=g_rows[B_HEADS + h:B_HEADS + h + 1, :]))
    ubuf[:, 0:8, :] = ubuf[:, C:C + 8, :]

    for un in units:
        un["qkk"] = _nt_dot(jnp.concatenate([un["kn"], un["qn"]], axis=0).astype(BF16), un["kb"])
    for un in units:
        dm = jnp.where(ri >= ci, jnp.exp(jnp.where(ri >= ci, un["g_col"] - un["g_row"], 0.0)), 0.0)
        a = jnp.where(ri > ci, un["beta"] * un["qkk"][0:C] * dm, 0.0)
        un["qk"] = (un["qkk"][C:2 * C] * dm).astype(BF16)
        un["ab"] = a.astype(BF16)
        un["t"] = eye_f - jnp.where(levels[0], a, 0.0)
    for lvl in levels[1:]:
        for un in units:
            un["tb"] = un["t"].astype(BF16)
            un["ta"] = f32dot(un["tb"], un["ab"]).astype(BF16)
        for un in units:
            un["t"] = un["t"] - jnp.where(lvl, f32dot(un["ta"], un["tb"]), 0.0)
    for un in units:
        e_g = jnp.exp(un["g_col"])
        rhs = jnp.concatenate([un["beta"] * un["vh"], un["beta"] * e_g * un["kn"]], axis=1).astype(BF16)
        un["uw"] = f32dot(un["t"].astype(BF16), rhs)
        un["qe"] = un["qn"] * e_g
    for un in units:
        s_old = s_ref[un["sq"], un["h"]]
        lhs = jnp.concatenate([un["uw"][:, B_HEAD_DIM:], un["qe"]], axis=0).astype(BF16)
        un["ws_qs"] = f32dot(lhs, s_old.astype(BF16))
    for un in units:
        sq, h = un["sq"], un["h"]
        wvb = (un["uw"][:, :B_HEAD_DIM] - un["ws_qs"][0:C]).astype(BF16)
        g_last = un["g_col"][C - 1:C, :]
        kd = (un["kn"] * jnp.exp(g_last - un["g_col"])).astype(BF16)
        o_ref[sq, :, h * B_HEAD_DIM:(h + 1) * B_HEAD_DIM] = un["ws_qs"][C:2 * C] + f32dot(un["qk"], wvb)
        s_ref[sq, h] = jnp.exp(g_last) * s_ref[sq, h] + _tn_dot(kd, wvb)


def _delta_rule(proj_main, conv0, s0, layer, conv_w, a_par, seq_len, chunk, nsb):
    rows, cols = proj_main.shape
    nseq = rows // seq_len
    nc = seq_len // chunk
    pm3 = proj_main.reshape(nseq, seq_len, cols)
    state_dims = (B_HEADS, B_HEAD_DIM, B_HEAD_DIM)
    o_b, s_new = pl.pallas_call(
        functools.partial(_delta_body, chunk=chunk, nsb=nsb),
        grid=(nseq // nsb, nc),
        in_specs=[pl.BlockSpec((nsb, chunk, CONV_CH), lambda n, c: (n, c, 0)),
                  pl.BlockSpec((nsb, chunk, BA_PAD), lambda n, c: (n, c, COL_BA)),
                  pl.BlockSpec((None, nsb, CONV_W - 1, CONV_CH), lambda n, c: (layer, n, 0, 0)),
                  pl.BlockSpec((None, nsb) + state_dims, lambda n, c: (layer, n, 0, 0, 0)),
                  pl.BlockSpec((CONV_W, CONV_CH), lambda n, c: (0, 0)),
                  pl.BlockSpec((2, BA_PAD), lambda n, c: (0, 0))],
        out_specs=[pl.BlockSpec((nsb, chunk, B_WIDTH), lambda n, c: (n, c, 0)),
                   pl.BlockSpec((nsb,) + state_dims, lambda n, c: (n, 0, 0, 0))],
        out_shape=[jax.ShapeDtypeStruct((nseq, seq_len, B_WIDTH), F32),
                   jax.ShapeDtypeStruct((nseq,) + state_dims, F32)],
        scratch_shapes=[pltpu.VMEM((nsb, chunk + 8, CONV_CH), F32)],
        compiler_params=_cparams("arbitrary", "arbitrary"),
        name="delta_c%d" % chunk,
    )(pm3, pm3, conv0, s0, conv_w, a_par)
    return o_b.reshape(rows, B_WIDTH), s_new


def _mix_out_body(x_ref, oa_ref, ob_ref, z_ref, ga_ref, gb_ref, g1_ref, sc2_ref, sh2_ref,
                  gn_ref, nf_ref, wa_ref, wb_ref, wo_ref, xo_ref, h2_ref, *, token_major):
    oa = jnp.concatenate([oa_ref[sl] for sl in range(SLABS)], axis=1)
    ya = jnp.dot(oa.astype(BF16), wa_ref[...], preferred_element_type=F32)
    gn = gn_ref[...]
    parts = []
    for h in range(B_HEADS):
        lo = h * B_HEAD_DIM
        t = ob_ref[:, lo:lo + B_HEAD_DIM]
        t = t * lax.rsqrt(jnp.mean(t * t, axis=-1, keepdims=True) + NORM_EPS) * gn
        parts.append((t * _silu(z_ref[:, lo:lo + B_HEAD_DIM])).astype(BF16))
    yb = jnp.dot(jnp.concatenate(parts, axis=1), wb_ref[...], preferred_element_type=F32)
    merged = _sigmoid(ga_ref[...]) * ya + _sigmoid(gb_ref[...]) * yb
    mix = jnp.dot(merged.astype(BF16), wo_ref[...], preferred_element_type=F32)
    xn = x_ref[...] + g1_ref[...] * mix
    xo_ref[...] = xn
    h2 = _rms_mod(xn, nf_ref[...], sc2_ref[...], sh2_ref[...])
    if token_major:
        for s in range(ROW_SLABS):
            h2_ref[pl.ds(s, xn.shape[0], stride=ROW_SLABS), :] = h2[:, s * LANES:(s + 1) * LANES]
    else:
        h2_ref[...] = h2.astype(BF16)


def _mixer_out(x, o_a, o_b, proj_main, g1, sc2, sh2, gdn_norm, norm_ffn, wa, wb, wo, *, tm, per_seq,
               token_major):
    rows, d = x.shape
    if token_major:
        h2_spec = pl.BlockSpec((tm * ROW_SLABS, LANES), lambda i: (i, 0))
        h2_shape = jax.ShapeDtypeStruct((rows * ROW_SLABS, LANES), F32)
    else:
        h2_spec = pl.BlockSpec((tm, d), lambda i: (i, 0))
        h2_shape = jax.ShapeDtypeStruct((rows, d), BF16)
    tps = (rows // g1.shape[0]) // tm if per_seq else None
    ms = _mod_spec(per_seq, tm, tps)

    def const(a):
        return pl.BlockSpec(a.shape, lambda i: (0,) * a.ndim)

    def colblk(c):
        return pl.BlockSpec((tm, d), lambda i: (i, c))

    return pl.pallas_call(
        functools.partial(_mix_out_body, token_major=token_major),
        grid=(rows // tm,),
        in_specs=[colblk(0), pl.BlockSpec((SLABS, tm, LANES), lambda i: (0, i, 0)), colblk(0),
                  colblk(COL_Z), colblk(COL_GA), colblk(COL_GB), ms, ms, ms,
                  const(gdn_norm), const(norm_ffn), const(wa), const(wb), const(wo)],
        out_specs=[colblk(0), h2_spec],
        out_shape=[jax.ShapeDtypeStruct((rows, d), F32), h2_shape],
        compiler_params=_cparams("arbitrary"),
        name="mixer_out",
    )(x, o_a, o_b, proj_main, proj_main, proj_main, g1, sc2, sh2, gdn_norm, norm_ffn, wa, wb, wo)


def _swiglu_accumulate(h, wg_ref, wu_ref, wd_ref, acc):
    gte = jnp.dot(h, wg_ref[...], preferred_element_type=F32)
    up = jnp.dot(h, wu_ref[...], preferred_element_type=F32)
    acc[...] += jnp.dot((_silu(gte) * up).astype(BF16), wd_ref[...], preferred_element_type=F32)


def _ffn_body(h_ref, base_ref, g2_ref, wg_ref, wu_ref, wd_ref, o_ref, acc):
    j = pl.program_id(1)

    @pl.when(j == 0)
    def _():
        acc[...] = jnp.zeros_like(acc)

    _swiglu_accumulate(h_ref[...], wg_ref, wu_ref, wd_ref, acc)

    @pl.when(j == pl.num_programs(1) - 1)
    def _():
        o_ref[...] = base_ref[...] + g2_ref[...] * acc[...]


def _ffn(h2, base, g2, w_up, w_down, *, tm, tf, per_seq):
    rows, d = h2.shape
    ff = w_down.shape[0]
    nj = ff // tf
    tps = (rows // g2.shape[0]) // tm if per_seq else None
    return pl.pallas_call(
        _ffn_body,
        grid=(rows // tm, nj),
        in_specs=[pl.BlockSpec((tm, d), lambda i, j: (i, 0)),
                  pl.BlockSpec((tm, d), lambda i, j: (i, 0)),
                  _mod_spec(per_seq, tm, tps),
                  pl.BlockSpec((d, tf), lambda i, j: (0, j)),
                  pl.BlockSpec((d, tf), lambda i, j: (0, j + nj)),
                  pl.BlockSpec((tf, d), lambda i, j: (j, 0))],
        out_specs=pl.BlockSpec((tm, d), lambda i, j: (i, 0)),
        out_shape=jax.ShapeDtypeStruct((rows, d), F32),
        scratch_shapes=[pltpu.VMEM((tm, d), F32)],
        compiler_params=_cparams("arbitrary", "arbitrary"),
        name="ffn",
    )(h2, base, g2, w_up, w_up, w_down)


R_I1, R_I2, R_W1, R_W2, R_R1, R_R2 = range(6)


def _router_body(x_ref, sc_ref, sh_ref, nf_ref, wr_ref, route_ref, cnt_ref, carry):
    @pl.when(pl.program_id(0) == 0)
    def _():
        carry[...] = jnp.zeros_like(carry)

    h = _rms_mod(x_ref[...], nf_ref[...], sc_ref[...], sh_ref[...])
    logits = jnp.dot(h, wr_ref[...], preferred_element_type=F32, precision=lax.Precision.HIGHEST)
    lane = lax.broadcasted_iota(jnp.int32, logits.shape, 1).astype(F32)
    width = float(logits.shape[1])
    logits = jnp.where(lane < N_EXPERTS, logits, NEG)
    m1 = jnp.max(logits, axis=1, keepdims=True)
    i1 = jnp.min(jnp.where(logits == m1, lane, width), axis=1, keepdims=True)
    rest = jnp.where(lane == i1, NEG, logits)
    m2 = jnp.max(rest, axis=1, keepdims=True)
    i2 = jnp.min(jnp.where(rest == m2, lane, width), axis=1, keepdims=True)
    e2 = jnp.exp(m2 - m1)
    w1 = 1.0 / (1.0 + e2)
    sel = jnp.where(lane == i1, 1.0, 0.0) + jnp.where(lane == i2, 1.0, 0.0)
    incl = _cumsum_rows(sel) + carry[...]
    excl = incl - sel
    r1 = jnp.sum(jnp.where(lane == i1, excl, 0.0), axis=1, keepdims=True)
    r2 = jnp.sum(jnp.where(lane == i2, excl, 0.0), axis=1, keepdims=True)
    total = incl[incl.shape[0] - 1:, :]
    carry[...] = total
    cnt_ref[...] = total
    route = jnp.zeros_like(logits)
    for k, v in ((R_I1, i1), (R_I2, i2), (R_W1, w1), (R_W2, e2 * w1), (R_R1, r1), (R_R2, r2)):
        route = jnp.where(lane == k, v, route)
    route_ref[...] = route


def _router(x, sc2, sh2, norm_ffn, w_router_pad, *, tm, per_seq):
    rows, d = x.shape
    tps = (rows // sc2.shape[0]) // tm if per_seq else None
    ms = _mod_spec(per_seq, tm, tps)
    width = w_router_pad.shape[1]
    return pl.pallas_call(
        _router_body,
        grid=(rows // tm,),
        in_specs=[pl.BlockSpec((tm, d), lambda i: (i, 0)), ms, ms,
                  pl.BlockSpec((1, d), lambda i: (0, 0)),
                  pl.BlockSpec((d, width), lambda i: (0, 0))],
        out_specs=[pl.BlockSpec((tm, width), lambda i: (i, 0)), pl.BlockSpec((1, width), lambda i: (0, 0))],
        out_shape=[jax.ShapeDtypeStruct((rows, width), F32), jax.ShapeDtypeStruct((1, width), F32)],
        scratch_shapes=[pltpu.VMEM((1, width), F32)],
        compiler_params=_cparams("arbitrary"),
        name="router",
    )(x, sc2, sh2, norm_ffn, w_router_pad)


def _route_plan(route, counts, tm):
    rows = route.shape[0]
    n_tiles = -(-2 * rows // tm) + N_EXPERTS + 1
    i1 = route[:, R_I1].astype(jnp.int32)
    i2 = route[:, R_I2].astype(jnp.int32)
    cnt = counts[0, :N_EXPERTS].astype(jnp.int32)
    tiles = (cnt + tm - 1) // tm
    ends = jnp.cumsum(tiles)
    starts = (ends - tiles) * tm
    pos1 = starts[i1] + route[:, R_R1].astype(jnp.int32)
    pos2 = starts[i2] + route[:, R_R2].astype(jnp.int32)
    tile_expert = jnp.minimum(jnp.sum(jnp.arange(n_tiles)[:, None] >= ends[None, :], axis=1), N_EXPERTS - 1)
    tok = jnp.arange(rows, dtype=jnp.int32)
    row_token = jnp.zeros((n_tiles * tm,), jnp.int32).at[jnp.concatenate([pos1, pos2])].set(
        jnp.concatenate([tok, tok]), unique_indices=True, mode="promise_in_bounds")
    return dict(n_tiles=n_tiles, tile_expert=tile_expert.astype(jnp.int32), n_used=ends[-1:].astype(jnp.int32),
                row_token=row_token.reshape(n_tiles, 1, tm), pos1=pos1, pos2=pos2)


def _start_row_gather(idx_ref, n, src_hbm, dst, sem):
    def body(r, carry):
        src = pl.multiple_of(idx_ref[0, 0, r] * ROW_SLABS, ROW_SLABS)
        pltpu.make_async_copy(src_hbm.at[pl.ds(src, ROW_SLABS), :],
                              dst.at[pl.ds(pl.multiple_of(r * ROW_SLABS, ROW_SLABS), ROW_SLABS), :], sem).start()
        return carry
    lax.fori_loop(0, n, body, 0, unroll=8)


def _wait_row_gather(n, src_hbm, dst, sem):
    pltpu.make_async_copy(src_hbm.at[pl.ds(0, n * ROW_SLABS), :], dst, sem).wait()


def _moe_ffn_body(te_ref, nu_ref, rt_ref, rtn_ref, h_hbm, wg_ref, wu_ref, wd_ref, o_ref,
                  xbuf, hbuf, acc, sem, *, tm, nj):
    i = pl.program_id(0)
    j = pl.program_id(1)
    used = i < nu_ref[0]
    slot = i % 2
    per_step = tm // nj

    @pl.when(jnp.logical_and(j == 0, i == 0))
    def _():
        _start_row_gather(rt_ref, tm, h_hbm, xbuf.at[0], sem.at[0])

    @pl.when(jnp.logical_and(j == 0, i <= nu_ref[0]))
    def _():
        _wait_row_gather(tm, h_hbm, xbuf.at[slot], sem.at[slot])

    @pl.when(jnp.logical_and(j == 0, used))
    def _():
        for s in range(ROW_SLABS):
            hbuf[:, s * LANES:(s + 1) * LANES] = xbuf[slot, pl.ds(s, tm, stride=ROW_SLABS), :].astype(BF16)
        acc[...] = jnp.zeros_like(acc)

    @pl.when(used)
    def _():
        for k in range(per_step):
            r = j * per_step + k
            src = pl.multiple_of(rtn_ref[0, 0, r] * ROW_SLABS, ROW_SLABS)
            pltpu.make_async_copy(
                h_hbm.at[pl.ds(src, ROW_SLABS), :],
                xbuf.at[1 - slot, pl.ds(pl.multiple_of(r * ROW_SLABS, ROW_SLABS), ROW_SLABS), :],
                sem.at[1 - slot]).start()
        _swiglu_accumulate(hbuf[...], wg_ref, wu_ref, wd_ref, acc)

    last = j == pl.num_programs(1) - 1

    @pl.when(jnp.logical_and(last, used))
    def _():
        for s in range(ROW_SLABS):
            o_ref[pl.ds(s, tm, stride=ROW_SLABS), :] = acc[:, s * LANES:(s + 1) * LANES]

    @pl.when(jnp.logical_and(last, jnp.logical_not(used)))
    def _():
        o_ref[...] = jnp.zeros_like(o_ref)


def _moe_ffn(h_rows, plan, w_up, w_down, moe_idx, *, tm, tf):
    d = w_up.shape[-2]
    ff = w_down.shape[-2]
    nj = ff // tf
    n_tiles = plan["n_tiles"]

    def jj(i, j, nu):
        return jnp.where(i < nu[0], j, nj - 1)

    grid_spec = pltpu.PrefetchScalarGridSpec(
        num_scalar_prefetch=2,
        grid=(n_tiles, nj),
        in_specs=[pl.BlockSpec((1, 1, tm), lambda i, j, te, nu: (i, 0, 0), memory_space=pltpu.SMEM),
                  pl.BlockSpec((1, 1, tm), lambda i, j, te, nu: (jnp.minimum(i + 1, n_tiles - 1), 0, 0),
                               memory_space=pltpu.SMEM),
                  pl.BlockSpec(memory_space=pl.ANY),
                  pl.BlockSpec((None, None, d, tf), lambda i, j, te, nu: (moe_idx, te[i], 0, jj(i, j, nu))),
                  pl.BlockSpec((None, None, d, tf), lambda i, j, te, nu: (moe_idx, te[i], 0, jj(i, j, nu) + nj)),
                  pl.BlockSpec((None, None, tf, d), lambda i, j, te, nu: (moe_idx, te[i], jj(i, j, nu), 0))],
        out_specs=pl.BlockSpec((tm * ROW_SLABS, LANES), lambda i, j, te, nu: (i, 0)),
        scratch_shapes=[pltpu.VMEM((2, tm * ROW_SLABS, LANES), F32), pltpu.VMEM((tm, d), BF16),
                        pltpu.VMEM((tm, d), F32), pltpu.SemaphoreType.DMA((2,))])
    return pl.pallas_call(
        functools.partial(_moe_ffn_body, tm=tm, nj=nj),
        grid_spec=grid_spec,
        out_shape=jax.ShapeDtypeStruct((n_tiles * tm * ROW_SLABS, LANES), F32),
        compiler_params=pltpu.CompilerParams(dimension_semantics=("arbitrary", "arbitrary"),
                                             vmem_limit_bytes=VMEM_LIMIT, disable_bounds_checks=True),
        name="moe_ffn",
    )(plan["tile_expert"], plan["n_used"], plan["row_token"], plan["row_token"], h_rows, w_up, w_up, w_down)


def _moe_combine_body(p1_ref, p2_ref, p1n_ref, p2n_ref, y_hbm, x_ref, g2_ref, route_ref, fw_ref, o_ref,
                      buf1, buf2, sem, *, tc, final):
    i = pl.program_id(0)
    slot = i % 2

    @pl.when(i == 0)
    def _():
        _start_row_gather(p1_ref, tc, y_hbm, buf1.at[0], sem.at[0, 0])
        _start_row_gather(p2_ref, tc, y_hbm, buf2.at[0], sem.at[1, 0])

    _wait_row_gather(tc, y_hbm, buf1.at[slot], sem.at[0, slot])
    _wait_row_gather(tc, y_hbm, buf2.at[slot], sem.at[1, slot])

    @pl.when(i + 1 < pl.num_programs(0))
    def _():
        _start_row_gather(p1n_ref, tc, y_hbm, buf1.at[1 - slot], sem.at[0, 1 - slot])
        _start_row_gather(p2n_ref, tc, y_hbm, buf2.at[1 - slot], sem.at[1, 1 - slot])

    w1 = route_ref[:, R_W1:R_W1 + 1]
    w2 = route_ref[:, R_W2:R_W2 + 1]
    pieces = []
    for s in range(ROW_SLABS):
        cols = slice(s * LANES, (s + 1) * LANES)
        y = (w1 * buf1[slot, pl.ds(s, tc, stride=ROW_SLABS), :]
             + w2 * buf2[slot, pl.ds(s, tc, stride=ROW_SLABS), :])
        pieces.append(x_ref[:, cols] + g2_ref[:, cols] * y)
    xn = jnp.concatenate(pieces, axis=1)
    if final:
        xn = xn * lax.rsqrt(jnp.mean(xn * xn, axis=-1, keepdims=True) + NORM_EPS) * fw_ref[...]
    o_ref[...] = xn


def _moe_combine(y_rows, plan, route, x, g2, final_w, *, tc, per_seq, final):
    rows, d = x.shape
    nt = rows // tc
    tps = (rows // g2.shape[0]) // tc if per_seq else None
    p1 = plan["pos1"].reshape(nt, 1, tc)
    p2 = plan["pos2"].reshape(nt, 1, tc)
    cur = pl.BlockSpec((1, 1, tc), lambda i: (i, 0, 0), memory_space=pltpu.SMEM)
    nxt = pl.BlockSpec((1, 1, tc), lambda i: (jnp.minimum(i + 1, nt - 1), 0, 0), memory_space=pltpu.SMEM)
    return pl.pallas_call(
        functools.partial(_moe_combine_body, tc=tc, final=final),
        grid=(nt,),
        in_specs=[cur, cur, nxt, nxt, pl.BlockSpec(memory_space=pl.ANY),
                  pl.BlockSpec((tc, d), lambda i: (i, 0)), _mod_spec(per_seq, tc, tps),
                  pl.BlockSpec((tc, route.shape[1]), lambda i: (i, 0)),
                  pl.BlockSpec((1, d), lambda i: (0, 0))],
        out_specs=pl.BlockSpec((tc, d), lambda i: (i, 0)),
        out_shape=jax.ShapeDtypeStruct((rows, d), F32),
        scratch_shapes=[pltpu.VMEM((2, tc * ROW_SLABS, LANES), F32), pltpu.VMEM((2, tc * ROW_SLABS, LANES), F32),
                        pltpu.SemaphoreType.DMA((2, 2))],
        compiler_params=pltpu.CompilerParams(dimension_semantics=("arbitrary",),
                                             vmem_limit_bytes=VMEM_LIMIT, disable_bounds_checks=True),
        name="moe_combine",
    )(p1, p2, p1, p2, y_rows, x, g2, route, final_w)


def _final_body(x_ref, w_ref, o_ref):
    x = x_ref[...]
    o_ref[...] = x * lax.rsqrt(jnp.mean(x * x, axis=-1, keepdims=True) + NORM_EPS) * w_ref[...]


def _final_norm(x, w, *, tm):
    rows, d = x.shape
    return pl.pallas_call(
        _final_body,
        grid=(rows // tm,),
        in_specs=[pl.BlockSpec((tm, d), lambda i: (i, 0)), pl.BlockSpec((1, d), lambda i: (0, 0))],
        out_specs=pl.BlockSpec((tm, d), lambda i: (i, 0)),
        out_shape=jax.ShapeDtypeStruct((rows, d), F32),
        compiler_params=_cparams("arbitrary"),
        name="final_norm",
    )(x, w)


def _rope_tables(pos):
    half = A_HEAD_DIM // 2
    inv = jnp.exp(-math.log(ROPE_THETA) * jnp.arange(half, dtype=F32) / half)
    ang = pos.astype(F32)[:, None] * inv[None, :]
    cos, sin = jnp.cos(ang), jnp.sin(ang)
    cos_t = jnp.tile(jnp.concatenate([cos, cos], axis=1), (1, HEADS_PER_SLAB))
    sin_t = jnp.tile(jnp.concatenate([-sin, sin], axis=1), (1, HEADS_PER_SLAB))
    return cos_t, sin_t


def _slab_rows(pa, c, nseq, seq_len, keep):
    if keep == seq_len:
        t = pa[c * SLABS:(c + 1) * SLABS]
    else:
        t = jnp.stack([lax.slice(pa, (c * SLABS, n * seq_len + seq_len - keep, 0),
                                 ((c + 1) * SLABS, (n + 1) * seq_len, LANES)) for n in range(nseq)], axis=1)
    t = t.reshape(SLABS, nseq, keep, HEADS_PER_SLAB, A_HEAD_DIM)
    return t.transpose(1, 2, 0, 3, 4).reshape(nseq, keep, A_HEADS, A_HEAD_DIM)


def _split_w_in(w):
    o_b = A_QKV
    o_z = o_b + CONV_CH
    o_ba = o_z + B_WIDTH
    o_g = o_ba + 2 * B_HEADS
    pad = jnp.zeros((w.shape[0], BA_PAD - 2 * B_HEADS), w.dtype)
    main = jnp.concatenate([w[:, o_b:o_ba], w[:, o_g:], w[:, o_ba:o_g], pad], axis=1)
    return w[:, :A_QKV].astype(BF16), main.astype(BF16)


def kernel(x_prompt, x_sample, cache_k_g0, cache_v_g0, cache_k_g1, cache_v_g1, cache_k_g2, cache_v_g2,
           state_conv, state_delta, c_prompt, c_sample, w_mod, b_mod, norm_mix, norm_ffn, w_in, conv_w,
           a_log, dt_bias, gdn_norm, w_branch_a, w_branch_b, w_out, w_ffn_up, w_ffn_down, w_router,
           w_exp_up, w_exp_down, final_norm):
    depth = w_in.shape[0]
    nb, seq, d = x_prompt.shape
    ns, dec, _ = x_sample.shape
    past = cache_k_g2.shape[2]

    c_all = jnp.concatenate([c_prompt, c_sample, jnp.zeros((-(nb + ns) % 8, d), F32)], axis=0)
    mod_all = _modulation(c_all, w_mod, b_mod)

    caches = [_window_minor(c) for c in (cache_k_g0, cache_v_g0, cache_k_g1, cache_v_g1, cache_k_g2, cache_v_g2)]
    tabs_p = _rope_tables(jnp.tile(jnp.arange(seq), nb))
    tabs_s = _rope_tables(jnp.tile(past + jnp.arange(dec), ns))
    zero_conv = jnp.zeros((depth, nb, CONV_W - 1, CONV_CH), F32)
    zero_state = jnp.zeros((depth, nb, B_HEADS, B_HEAD_DIM, B_HEAD_DIM), F32)

    exp_up = w_exp_up.astype(BF16)
    exp_down = w_exp_down.astype(BF16)
    xp = x_prompt.reshape(nb * seq, d)
    xs = x_sample.reshape(ns * dec, d)
    new_p = [[] for _ in range(8)]
    new_s = [[] for _ in range(8)]
    tm_p, tm_s = 512, 512

    for l in range(depth):
        mods_p = [m.reshape(nb, 1, d) for m in jnp.split(mod_all[l, :nb], 6, axis=-1)]
        mods_s = [jnp.repeat(m, dec, axis=0) for m in jnp.split(mod_all[l, nb:nb + ns], 6, axis=-1)]
        w_attn, w_main = _split_w_in(w_in[l])
        nw_mix = norm_mix[l].reshape(1, d)
        nw_ffn = norm_ffn[l].reshape(1, d)
        a_par = jnp.zeros((2, BA_PAD), F32)
        a_par = a_par.at[0, B_HEADS:2 * B_HEADS].set(a_log[l]).at[1, B_HEADS:2 * B_HEADS].set(dt_bias[l])
        wa = w_branch_a[l].astype(BF16)
        wb = w_branch_b[l].astype(BF16)
        wo = w_out[l].astype(BF16)
        gn = gdn_norm[l].reshape(1, B_HEAD_DIM)

        outs = []
        for (x, mods, per_seq, tm, tabs, is_prompt) in ((xp, mods_p, True, tm_p, tabs_p, True),
                                                        (xs, mods_s, False, tm_s, tabs_s, False)):
            sh1, sc1, g1, sh2, sc2, g2 = mods
            tm_proj = min(x.shape[0], PROJ_ROWS)
            pm = _project(x, sc1, sh1, nw_mix, w_main, tm=tm_proj, tn=MAIN_COLS // 5, per_seq=per_seq,
                          name="proj_main")
            pa = _project(x, sc1, sh1, nw_mix, w_attn, tm=tm_proj, tn=3 * A_WIDTH, per_seq=per_seq,
                          rope_tabs=tabs, name="proj_attn")
            if is_prompt:
                run = None
                for gi in range(N_GROUPS):
                    run = _band_attention(pa, gi, seq, run)
                o_a = run[0]
                o_b, s_new = _delta_rule(pm, zero_conv, zero_state, l, conv_w[l], a_par, seq, PROMPT_CHUNK, nb)
            else:
                o_a = _cached_attention(pa, caches, l, dec)
                o_b, s_new = _delta_rule(pm, state_conv, state_delta, l, conv_w[l], a_par, dec, dec, SAMPLE_SEQS_PER_STEP)
            x, h2 = _mixer_out(x, o_a, o_b, pm, g1, sc2, sh2, gn, nw_ffn, wa, wb, wo, tm=tm, per_seq=per_seq,
                               token_major=l % 2 == 1)
            if l % 2 == 0:
                x = _ffn(h2, x, g2, w_ffn_up[l // 2].astype(BF16), w_ffn_down[l // 2].astype(BF16),
                         tm=tm, tf=1408, per_seq=per_seq)
            else:
                wr = jnp.zeros((d, LANES), F32).at[:, :N_EXPERTS].set(w_router[l // 2])
                route, counts = _router(x, sc2, sh2, nw_ffn, wr, tm=tm, per_seq=per_seq)
                tm_moe = MOE_ROWS if x.shape[0] >= 8 * MOE_ROWS else MOE_ROWS // 2
                plan = _route_plan(route, counts, tm_moe)
                y_rows = _moe_ffn(h2, plan, exp_up, exp_down, l // 2, tm=tm_moe, tf=1792)
                x = _moe_combine(y_rows, plan, route, x, g2, final_norm.reshape(1, d), tc=min(512, x.shape[0]),
                                 per_seq=per_seq, final=l == depth - 1)
            outs.append((x, pa, pm, s_new))

        (xp, pa_p, pm_p, s_p), (xs, pa_s, pm_s, s_s) = outs
        for gi, (win, dil) in enumerate(ATTN_GROUPS):
            keep = min(win, seq)
            new_p[2 * gi].append(_slab_rows(pa_p, 3 * gi + 1, nb, seq, keep))
            new_p[2 * gi + 1].append(_slab_rows(pa_p, 3 * gi + 2, nb, seq, keep))
            new_s[2 * gi].append(_slab_rows(pa_s, 3 * gi + 1, ns, dec, dec))
            new_s[2 * gi + 1].append(_slab_rows(pa_s, 3 * gi + 2, ns, dec, dec))
        new_p[6].append(pm_p.reshape(nb, seq, MAIN_COLS)[:, seq - (CONV_W - 1):, :CONV_CH])
        new_p[7].append(s_p)
        conv_s = jnp.concatenate([state_conv[l], pm_s.reshape(ns, dec, MAIN_COLS)[:, :, :CONV_CH]], axis=1)
        new_s[6].append(conv_s[:, -(CONV_W - 1):])
        new_s[7].append(s_s)

    if depth % 2 == 1:
        xp = _final_norm(xp, final_norm.reshape(1, d), tm=tm_p)
        xs = _final_norm(xs, final_norm.reshape(1, d), tm=tm_s)
    y_prompt = xp.reshape(nb, seq, d)
    y_sample = xs.reshape(ns, dec, d)
    return (y_prompt, y_sample, *[jnp.stack(t) for t in new_p], *[jnp.stack(t) for t in new_s])
```

```python
import functools
import math

import numpy as np
import jax
import jax.numpy as jnp
from jax import lax
from jax.experimental import pallas as pl
from jax.experimental.pallas import tpu as pltpu

F32 = jnp.float32
BF16 = jnp.bfloat16

D_MODEL = 1024
ATTN_GROUPS = ((128, 1), (512, 4), (2048, 16))
N_GROUPS = 3
A_HEADS = 4
A_HEAD_DIM = 64
A_WIDTH = A_HEADS * A_HEAD_DIM
LANES = 128
SLABS = A_WIDTH // LANES
HEADS_PER_SLAB = LANES // A_HEAD_DIM
ROW_SLABS = D_MODEL // LANES
ROPE_THETA = 10000.0
BAND_BLOCK = 128
CACHED_SEQS_PER_STEP = 2
BAND_TILING = ((8, 1), (4, 1), (1, 4))
B_HEADS = 8
B_HEAD_DIM = 128
B_WIDTH = B_HEADS * B_HEAD_DIM
CONV_W = 4
CONV_CH = 3 * B_WIDTH
PROMPT_CHUNK = 128
SAMPLE_SEQS_PER_STEP = 4
PROJ_ROWS = 1024
MOE_ROWS = 512
N_EXPERTS = 8
NORM_EPS = 1e-6
A_QKV = N_GROUPS * 3 * A_WIDTH
NEG = -1e30

BA_PAD = 256
MAIN_COLS = CONV_CH + 3 * D_MODEL + BA_PAD
COL_Z = CONV_CH // D_MODEL
COL_GA = COL_Z + 1
COL_GB = COL_Z + 2
COL_BA = (CONV_CH + 3 * D_MODEL) // BA_PAD

VMEM_LIMIT = 52 * 1024 * 1024


def _cparams(*sem):
    return pltpu.CompilerParams(dimension_semantics=sem, vmem_limit_bytes=VMEM_LIMIT)


def _sigmoid(x):
    return 1.0 / (1.0 + jnp.exp(-x))


def _silu(x):
    return x * _sigmoid(x)


def _nt_dot(a, b):
    return lax.dot_general(a, b, (((1,), (1,)), ((), ())), preferred_element_type=F32)


def _tn_dot(a, b):
    return lax.dot_general(a, b, (((0,), (0,)), ((), ())), preferred_element_type=F32)


def _mod_spec(per_seq, tm, tiles_per_seq):
    if per_seq:
        return pl.BlockSpec((None, 1, D_MODEL), lambda i, *_: (i // tiles_per_seq, 0, 0))
    return pl.BlockSpec((tm, D_MODEL), lambda i, *_: (i, 0))


def _rms_mod(x, nw, sc, sh):
    y = x * lax.rsqrt(jnp.mean(x * x, axis=-1, keepdims=True) + NORM_EPS) * nw
    return y * (1.0 + sc) + sh


def _mod_body(c_ref, w_ref, b_ref, o_ref):
    s = _silu(c_ref[...]).astype(BF16)
    o_ref[...] = jnp.dot(s, w_ref[...].astype(BF16), preferred_element_type=F32) + b_ref[...]


def _modulation(c_all, w_mod, b_mod):
    depth, d, n = w_mod.shape
    rows = c_all.shape[0]
    tn = 1536
    return pl.pallas_call(
        _mod_body,
        grid=(depth, n // tn),
        in_specs=[pl.BlockSpec((rows, d), lambda l, j: (0, 0)),
                  pl.BlockSpec((None, d, tn), lambda l, j: (l, 0, j)),
                  pl.BlockSpec((None, 1, tn), lambda l, j: (l, 0, j))],
        out_specs=pl.BlockSpec((None, rows, tn), lambda l, j: (l, 0, j)),
        out_shape=jax.ShapeDtypeStruct((depth, rows, n), F32),
        compiler_params=_cparams("arbitrary", "arbitrary"),
        name="adaln_mod",
    )(c_all, w_mod, b_mod.reshape(depth, 1, n))


def _proj_body(x_ref, sc_ref, sh_ref, nw_ref, w_ref, *rest, rope):
    if rope:
        cos_ref, sin_ref, o_ref, h_scr = rest
    else:
        o_ref, h_scr = rest

    @pl.when(pl.program_id(1) == 0)
    def _():
        h_scr[...] = _rms_mod(x_ref[...], nw_ref[...], sc_ref[...], sh_ref[...]).astype(BF16)

    y = jnp.dot(h_scr[...], w_ref[...], preferred_element_type=F32)
    if not rope:
        o_ref[...] = y
        return
    cos = cos_ref[...]
    sin = sin_ref[...]
    lane = lax.broadcasted_iota(jnp.int32, cos.shape, 1)
    first_half = (lane % A_HEAD_DIM) < (A_HEAD_DIM // 2)
    for c in range(3 * SLABS):
        t = y[:, c * LANES:(c + 1) * LANES]
        if c < 2 * SLABS:
            partner = jnp.where(first_half, pltpu.roll(t, LANES - A_HEAD_DIM // 2, 1),
                                pltpu.roll(t, A_HEAD_DIM // 2, 1))
            t = t * cos + partner * sin
        o_ref[c] = t


def _project(x, sc, sh, nw, w, *, tm, tn, per_seq, rope_tabs=None, name):
    rows, d = x.shape
    n = w.shape[1]
    tps = (rows // sc.shape[0]) // tm if per_seq else None
    in_specs = [pl.BlockSpec((tm, d), lambda i, j: (i, 0)),
                _mod_spec(per_seq, tm, tps), _mod_spec(per_seq, tm, tps),
                pl.BlockSpec((1, d), lambda i, j: (0, 0)),
                pl.BlockSpec((d, tn), lambda i, j: (0, j))]
    args = [x, sc, sh, nw, w]
    out_spec = pl.BlockSpec((tm, tn), lambda i, j: (i, j))
    out_shape = jax.ShapeDtypeStruct((rows, n), F32)
    if rope_tabs is not None:
        in_specs += [pl.BlockSpec((tm, LANES), lambda i, j: (i, 0))] * 2
        args += list(rope_tabs)
        out_spec = pl.BlockSpec((tn // LANES, tm, LANES), lambda i, j: (j, i, 0))
        out_shape = jax.ShapeDtypeStruct((n // LANES, rows, LANES), F32)
    return pl.pallas_call(
        functools.partial(_proj_body, rope=rope_tabs is not None),
        grid=(rows // tm, n // tn),
        in_specs=in_specs,
        out_specs=out_spec,
        out_shape=out_shape,
        scratch_shapes=[pltpu.VMEM((tm, d), BF16)],
        compiler_params=_cparams("arbitrary", "arbitrary"),
        name=name,
    )(*args)


def _head_masks(width):
    lane = lax.broadcasted_iota(jnp.int32, (1, width), 1)
    return [(lane // A_HEAD_DIM == h).astype(F32) for h in range(A_HEADS)]


def _merge_lse(o_a, l_a, o_b, l_b):
    m = jnp.maximum(l_a, l_b)
    e_a = jnp.exp(l_a - m)
    e_b = jnp.exp(l_b - m)
    den = e_a + e_b
    return (o_a * e_a + o_b * e_b) / den, m + jnp.log(den)


def _band_body(q_ref, kc_ref, kp_ref, vc_ref, vp_ref, bias_ref, *rest, dil, nb, res_per_iter, has_prev_group):
    if has_prev_group:
        oprev_ref, lprev_ref, o_ref, l_ref = rest
    else:
        o_ref, l_ref = rest
    blk = BAND_BLOCK
    col = lax.broadcasted_iota(jnp.int32, (blk, 2 * blk), 1)
    no_prev = jnp.logical_and(pl.program_id(1) == 0, col < blk)
    bias_h = jnp.concatenate([bias_ref[...]] * HEADS_PER_SLAB, axis=0)
    bias_h0 = jnp.concatenate([jnp.where(no_prev, NEG, bias_ref[...])] * HEADS_PER_SLAB, axis=0)
    hm = _head_masks(LANES)

    def rows_of(r, b):
        return pl.ds(r + dil * blk * b, blk, stride=dil) if dil > 1 else pl.ds(blk * b, blk)

    def with_prev(cur_ref, prev_ref, sl, r, b):
        prev = prev_ref[sl, rows_of(r, 0), :] if b == 0 else cur_ref[sl, rows_of(r, b - 1), :]
        return jnp.concatenate([prev, cur_ref[sl, rows_of(r, b), :]], axis=0).astype(BF16)

    def process(residues):
        units = [dict(r=r, b=b, sl=sl) for r in residues for b in range(nb) for sl in range(SLABS)]
        for un in units:
            q = q_ref[un["sl"], rows_of(un["r"], un["b"]), :] * (A_HEAD_DIM ** -0.5)
            qh = jnp.concatenate([q * hm[h] for h in range(HEADS_PER_SLAB)], axis=0).astype(BF16)
            k = with_prev(kc_ref, kp_ref, un["sl"], un["r"], un["b"])
            un["s"] = _nt_dot(qh, k) + (bias_h0 if un["b"] == 0 else bias_h)
        for un in units:
            m = jnp.max(un["s"], axis=1, keepdims=True)
            p = jnp.exp(un["s"] - m)
            un["l"] = jnp.sum(p, axis=1, keepdims=True)
            un["m"] = m
            un["p"] = p.astype(BF16)
        for un in units:
            v = with_prev(vc_ref, vp_ref, un["sl"], un["r"], un["b"])
            un["oh"] = jnp.dot(un["p"], v, preferred_element_type=F32)
        for un in units:
            rows = rows_of(un["r"], un["b"])
            oh = un["oh"] / un["l"]
            lseh = un["m"] + jnp.log(un["l"])
            o = sum(oh[h * blk:(h + 1) * blk] * hm[h] for h in range(HEADS_PER_SLAB))
            lse = sum(lseh[h * blk:(h + 1) * blk] * hm[h] for h in range(HEADS_PER_SLAB))
            if has_prev_group:
                o, lse = _merge_lse(oprev_ref[un["sl"], rows, :], lprev_ref[un["sl"], rows, :], o, lse)
            o_ref[un["sl"], rows, :] = o
            l_ref[un["sl"], rows, :] = lse

    if dil <= res_per_iter:
        process(list(range(dil)))
    else:
        def loop_body(it, carry):
            process([it * res_per_iter + k for k in range(res_per_iter)])
            return carry
        lax.fori_loop(0, dil // res_per_iter, loop_body, 0)


def _band_bias(nk):
    qi = np.arange(BAND_BLOCK)[:, None]
    kj = np.arange(2 * BAND_BLOCK)[None, :]
    rel = kj - BAND_BLOCK - qi
    return jnp.asarray(np.where((rel <= 0) & (rel >= -nk), 0.0, NEG), F32)


def _band_attention(qkv, gi, seq_len, prev):
    win, dil = ATTN_GROUPS[gi]
    rows = qkv.shape[1]
    nb, res_per_iter = BAND_TILING[gi]
    blk_rows = BAND_BLOCK * dil
    nb = min(nb, seq_len // blk_rows)
    sb_rows = blk_rows * nb
    nsb = seq_len // sb_rows
    nseq = rows // seq_len

    def cur(c):
        return pl.BlockSpec((SLABS, sb_rows, LANES), lambda n, s: (c, n * nsb + s, 0))

    def prv(c):
        return pl.BlockSpec((SLABS, blk_rows, LANES),
                            lambda n, s: (c, jnp.maximum((n * nsb + s) * nb - 1, n * nsb * nb), 0))

    row_spec = pl.BlockSpec((SLABS, sb_rows, LANES), lambda n, s: (0, n * nsb + s, 0))
    in_specs = [cur(3 * gi), cur(3 * gi + 1), prv(3 * gi + 1), cur(3 * gi + 2), prv(3 * gi + 2),
                pl.BlockSpec((BAND_BLOCK, 2 * BAND_BLOCK), lambda n, s: (0, 0))]
    args = [qkv, qkv, qkv, qkv, qkv, _band_bias(win // dil)]
    if prev is not None:
        in_specs += [row_spec, row_spec]
        args += list(prev)
    return pl.pallas_call(
        functools.partial(_band_body, dil=dil, nb=nb, res_per_iter=res_per_iter, has_prev_group=prev is not None),
        grid=(nseq, nsb),
        in_specs=in_specs,
        out_specs=[row_spec, row_spec],
        out_shape=[jax.ShapeDtypeStruct((SLABS, rows, LANES), F32)] * 2,
        compiler_params=_cparams("arbitrary", "arbitrary"),
        name="band_attn_g%d" % gi,
    )(*args)


def _cached_body(row_ref, *rest, n_new, nsq):
    caches = rest[:2 * N_GROUPS]
    biases = rest[2 * N_GROUPS:4 * N_GROUPS]
    o_ref = rest[4 * N_GROUPS]
    hm = _head_masks(LANES)
    units = [dict(sq=sq, sl=sl, gi=gi) for sq in range(nsq) for sl in range(SLABS) for gi in range(N_GROUPS)]
    for un in units:
        sl, gi = un["sl"], un["gi"]
        un["rows"] = slice(un["sq"] * n_new, (un["sq"] + 1) * n_new)
        q = row_ref[(3 * gi) * SLABS + sl, un["rows"], :] * (A_HEAD_DIM ** -0.5)
        kn = row_ref[(3 * gi + 1) * SLABS + sl, un["rows"], :].astype(BF16)
        qh = jnp.concatenate([q * hm[h] for h in range(HEADS_PER_SLAB)], axis=0).astype(BF16)
        kct = caches[2 * gi][un["sq"], sl].astype(BF16)
        un["s_c"] = jnp.dot(qh, kct, preferred_element_type=F32) + biases[2 * gi][...]
        un["s_n"] = _nt_dot(qh, kn) + biases[2 * gi + 1][...]
    for un in units:
        m = jnp.maximum(jnp.max(un["s_c"], axis=1, keepdims=True), jnp.max(un["s_n"], axis=1, keepdims=True))
        p_c = jnp.exp(un["s_c"] - m)
        p_n = jnp.exp(un["s_n"] - m)
        un["l"] = jnp.sum(p_c, axis=1, keepdims=True) + jnp.sum(p_n, axis=1, keepdims=True)
        un["m"] = m
        un["p_c"] = p_c.astype(BF16)
        un["p_n"] = p_n.astype(BF16)
    for un in units:
        sl, gi = un["sl"], un["gi"]
        vn = row_ref[(3 * gi + 2) * SLABS + sl, un["rows"], :].astype(BF16)
        vct = caches[2 * gi + 1][un["sq"], sl].astype(BF16)
        un["oh"] = _nt_dot(un["p_c"], vct) + jnp.dot(un["p_n"], vn, preferred_element_type=F32)
    for first in range(0, len(units), N_GROUPS):
        o_run = l_run = None
        for un in units[first:first + N_GROUPS]:
            oh = un["oh"] / un["l"]
            lseh = un["m"] + jnp.log(un["l"])
            o = sum(oh[h * n_new:(h + 1) * n_new] * hm[h] for h in range(HEADS_PER_SLAB))
            lse = sum(lseh[h * n_new:(h + 1) * n_new] * hm[h] for h in range(HEADS_PER_SLAB))
            if un["gi"] == 0:
                o_run, l_run = o, lse
            else:
                o_run, l_run = _merge_lse(o_run, l_run, o, lse)
        o_ref[un["sl"], un["rows"], :] = o_run


def _cached_bias(wb, n_new, win, dil):
    nk = win // dil
    t = np.tile(np.arange(n_new), HEADS_PER_SLAB)[:, None]
    delta = wb + t - np.arange(wb + n_new)[None, :]
    ok = (delta >= 0) & (delta % dil == 0) & (delta <= dil * nk)
    b = np.where(ok, 0.0, NEG).astype(np.float32)
    return jnp.asarray(b[:, :wb]), jnp.asarray(b[:, wb:])


def _window_minor(c):
    depth, n, wb = c.shape[:3]
    return c.transpose(0, 1, 3, 4, 2).reshape(depth, n, SLABS, LANES, wb)


def _cached_attention(qkv, caches, layer, n_new):
    nslab, rows, _ = qkv.shape
    nseq = rows // n_new
    nsq = CACHED_SEQS_PER_STEP if nseq % CACHED_SEQS_PER_STEP == 0 else 1
    in_specs = [pl.BlockSpec((nslab, nsq * n_new, LANES), lambda n: (0, n, 0))]
    args = [qkv]
    for c in caches:
        in_specs.append(pl.BlockSpec((None, nsq, SLABS, LANES, c.shape[4]), lambda n: (layer, n, 0, 0, 0)))
        args.append(c)
    for gi, (win, dil) in enumerate(ATTN_GROUPS):
        for b in _cached_bias(caches[2 * gi].shape[4], n_new, win, dil):
            in_specs.append(pl.BlockSpec(b.shape, lambda n: (0, 0)))
            args.append(b)
    return pl.pallas_call(
        functools.partial(_cached_body, n_new=n_new, nsq=nsq),
        grid=(nseq // nsq,),
        in_specs=in_specs,
        out_specs=pl.BlockSpec((SLABS, nsq * n_new, LANES), lambda n: (0, n, 0)),
        out_shape=jax.ShapeDtypeStruct((SLABS, rows, LANES), F32),
        compiler_params=_cparams("arbitrary"),
        name="cached_attn",
    )(*args)


def _cumsum_rows(x):
    n = x.shape[0]
    row = lax.broadcasted_iota(jnp.int32, x.shape, 0)
    sh = 1
    while sh < n:
        x = x + jnp.where(row >= sh, pltpu.roll(x, sh, 0), 0.0)
        sh *= 2
    return x


def _delta_body(u_ref, ba_ref, conv0_ref, s0_ref, cw_ref, ap_ref, o_ref, s_ref, ubuf, *, chunk, nsb):
    C = chunk
    f32dot = functools.partial(jnp.dot, preferred_element_type=F32)

    @pl.when(pl.program_id(1) == 0)
    def _():
        ubuf[:, 0:8, :] = jnp.zeros((nsb, 8, CONV_CH), F32)
        ubuf[:, 8 - (CONV_W - 1):8, :] = conv0_ref[...]
        s_ref[...] = s0_ref[...]

    ubuf[:, 8:8 + C, :] = u_ref[...]
    cw = cw_ref[...]
    ap = ap_ref[...]
    ri = lax.broadcasted_iota(jnp.int32, (C, C), 0)
    ci = lax.broadcasted_iota(jnp.int32, (C, C), 1)
    eye_f = (ri == ci).astype(F32)
    levels = []
    b = 1
    while b < C:
        levels.append(((ri // (2 * b)) == (ci // (2 * b))) & ((ri // b) % 2 == 1) & ((ci // b) % 2 == 0))
        b *= 2

    units = []
    for sq in range(nsb):
        conv = ubuf[sq, 8:8 + C, :] * cw[CONV_W - 1:CONV_W]
        for i in range(CONV_W - 1):
            off = 8 - (CONV_W - 1) + i
            conv = conv + ubuf[sq, off:off + C, :] * cw[i:i + 1]
        cu = _silu(conv)
        ba = ba_ref[sq]
        beta_f = _sigmoid(ba)
        z = ba + ap[1:2]
        softplus = jnp.maximum(z, 0.0) + jnp.log(1.0 + jnp.exp(-jnp.abs(z)))
        g_cum = _cumsum_rows(-jnp.exp(ap[0:1]) * softplus)
        g_rows = g_cum[:, 0:LANES].T
        for h in range(B_HEADS):
            lo = h * B_HEAD_DIM
            qh = cu[:, lo:lo + B_HEAD_DIM]
            kh = cu[:, B_WIDTH + lo:B_WIDTH + lo + B_HEAD_DIM]
            qn = qh * lax.rsqrt(jnp.sum(qh * qh, axis=-1, keepdims=True) + NORM_EPS) * (B_HEAD_DIM ** -0.5)
            kn = kh * lax.rsqrt(jnp.sum(kh * kh, axis=-1, keepdims=True) + NORM_EPS)
            g_col = g_cum[:, B_HEADS + h:B_HEADS + h + 1]
            units.append(dict(sq=sq, h=h, qn=qn, kn=kn, kb=kn.astype(BF16),
                              vh=cu[:, 2 * B_WIDTH + lo:2 * B_WIDTH + lo + B_HEAD_DIM],
                              beta=beta_f[:, h:h + 1], g_col=g_col,
                              g_row=g_rows[B_HEADS + h:B_HEADS + h + 1, :]))
    ubuf[:, 0:8, :] = ubuf[:, C:C + 8, :]

    for un in units:
        un["qkk"] = _nt_dot(jnp.concatenate([un["kn"], un["qn"]], axis=0).astype(BF16), un["kb"])
    for un in units:
        dm = jnp.where(ri >= ci, jnp.exp(jnp.where(ri >= ci, un["g_col"] - un["g_row"], 0.0)), 0.0)
        a = jnp.where(ri > ci, un["beta"] * un["qkk"][0:C] * dm, 0.0)
        un["qk"] = (un["qkk"][C:2 * C] * dm).astype(BF16)
        un["ab"] = a.astype(BF16)
        un["t"] = eye_f - jnp.where(levels[0], a, 0.0)
    for lvl in levels[1:]:
        for un in units:
            un["tb"] = un["t"].astype(BF16)
            un["ta"] = f32dot(un["tb"], un["ab"]).astype(BF16)
        for un in units:
            un["t"] = un["t"] - jnp.where(lvl, f32dot(un["ta"], un["tb"]), 0.0)
    for un in units:
        e_g = jnp.exp(un["g_col"])
        rhs = jnp.concatenate([un["beta"] * un["vh"], un["beta"] * e_g * un["kn"]], axis=1).astype(BF16)
        un["uw"] = f32dot(un["t"].astype(BF16), rhs)
        un["qe"] = un["qn"] * e_g
    for un in units:
        s_old = s_ref[un["sq"], un["h"]]
        lhs = jnp.concatenate([un["uw"][:, B_HEAD_DIM:], un["qe"]], axis=0).astype(BF16)
        un["ws_qs"] = f32dot(lhs, s_old.astype(BF16))
    for un in units:
        sq, h = un["sq"], un["h"]
        wvb = (un["uw"][:, :B_HEAD_DIM] - un["ws_qs"][0:C]).astype(BF16)
        g_last = un["g_col"][C - 1:C, :]
        kd = (un["kn"] * jnp.exp(g_last - un["g_col"])).astype(BF16)
        o_ref[sq, :, h * B_HEAD_DIM:(h + 1) * B_HEAD_DIM] = un["ws_qs"][C:2 * C] + f32dot(un["qk"], wvb)
        s_ref[sq, h] = jnp.exp(g_last) * s_ref[sq, h] + _tn_dot(kd, wvb)


def _delta_rule(proj_main, conv0, s0, layer, conv_w, a_par, seq_len, chunk, nsb):
    rows, cols = proj_main.shape
    nseq = rows // seq_len
    nc = seq_len // chunk
    pm3 = proj_main.reshape(nseq, seq_len, cols)
    state_dims = (B_HEADS, B_HEAD_DIM, B_HEAD_DIM)
    o_b, s_new = pl.pallas_call(
        functools.partial(_delta_body, chunk=chunk, nsb=nsb),
        grid=(nseq // nsb, nc),
        in_specs=[pl.BlockSpec((nsb, chunk, CONV_CH), lambda n, c: (n, c, 0)),
                  pl.BlockSpec((nsb, chunk, BA_PAD), lambda n, c: (n, c, COL_BA)),
                  pl.BlockSpec((None, nsb, CONV_W - 1, CONV_CH), lambda n, c: (layer, n, 0, 0)),
                  pl.BlockSpec((None, nsb) + state_dims, lambda n, c: (layer, n, 0, 0, 0)),
                  pl.BlockSpec((CONV_W, CONV_CH), lambda n, c: (0, 0)),
                  pl.BlockSpec((2, BA_PAD), lambda n, c: (0, 0))],
        out_specs=[pl.BlockSpec((nsb, chunk, B_WIDTH), lambda n, c: (n, c, 0)),
                   pl.BlockSpec((nsb,) + state_dims, lambda n, c: (n, 0, 0, 0))],
        out_shape=[jax.ShapeDtypeStruct((nseq, seq_len, B_WIDTH), F32),
                   jax.ShapeDtypeStruct((nseq,) + state_dims, F32)],
        scratch_shapes=[pltpu.VMEM((nsb, chunk + 8, CONV_CH), F32)],
        compiler_params=_cparams("arbitrary", "arbitrary"),
        name="delta_c%d" % chunk,
    )(pm3, pm3, conv0, s0, conv_w, a_par)
    return o_b.reshape(rows, B_WIDTH), s_new


def _mix_out_body(x_ref, oa_ref, ob_ref, z_ref, ga_ref, gb_ref, g1_ref, sc2_ref, sh2_ref,
                  gn_ref, nf_ref, wa_ref, wb_ref, wo_ref, xo_ref, h2_ref, *, token_major):
    oa = jnp.concatenate([oa_ref[sl] for sl in range(SLABS)], axis=1)
    ya = jnp.dot(oa.astype(BF16), wa_ref[...], preferred_element_type=F32)
    gn = gn_ref[...]
    parts = []
    for h in range(B_HEADS):
        lo = h * B_HEAD_DIM
        t = ob_ref[:, lo:lo + B_HEAD_DIM]
        t = t * lax.rsqrt(jnp.mean(t * t, axis=-1, keepdims=True) + NORM_EPS) * gn
        parts.append((t * _silu(z_ref[:, lo:lo + B_HEAD_DIM])).astype(BF16))
    yb = jnp.dot(jnp.concatenate(parts, axis=1), wb_ref[...], preferred_element_type=F32)
    merged = _sigmoid(ga_ref[...]) * ya + _sigmoid(gb_ref[...]) * yb
    mix = jnp.dot(merged.astype(BF16), wo_ref[...], preferred_element_type=F32)
    xn = x_ref[...] + g1_ref[...] * mix
    xo_ref[...] = xn
    h2 = _rms_mod(xn, nf_ref[...], sc2_ref[...], sh2_ref[...])
    if token_major:
        for s in range(ROW_SLABS):
            h2_ref[pl.ds(s, xn.shape[0], stride=ROW_SLABS), :] = h2[:, s * LANES:(s + 1) * LANES]
    else:
        h2_ref[...] = h2.astype(BF16)


def _mixer_out(x, o_a, o_b, proj_main, g1, sc2, sh2, gdn_norm, norm_ffn, wa, wb, wo, *, tm, per_seq,
               token_major):
    rows, d = x.shape
    if token_major:
        h2_spec = pl.BlockSpec((tm * ROW_SLABS, LANES), lambda i: (i, 0))
        h2_shape = jax.ShapeDtypeStruct((rows * ROW_SLABS, LANES), F32)
    else:
        h2_spec = pl.BlockSpec((tm, d), lambda i: (i, 0))
        h2_shape = jax.ShapeDtypeStruct((rows, d), BF16)
    tps = (rows // g1.shape[0]) // tm if per_seq else None
    ms = _mod_spec(per_seq, tm, tps)

    def const(a):
        return pl.BlockSpec(a.shape, lambda i: (0,) * a.ndim)

    def colblk(c):
        return pl.BlockSpec((tm, d), lambda i: (i, c))

    return pl.pallas_call(
        functools.partial(_mix_out_body, token_major=token_major),
        grid=(rows // tm,),
        in_specs=[colblk(0), pl.BlockSpec((SLABS, tm, LANES), lambda i: (0, i, 0)), colblk(0),
                  colblk(COL_Z), colblk(COL_GA), colblk(COL_GB), ms, ms, ms,
                  const(gdn_norm), const(norm_ffn), const(wa), const(wb), const(wo)],
        out_specs=[colblk(0), h2_spec],
        out_shape=[jax.ShapeDtypeStruct((rows, d), F32), h2_shape],
        compiler_params=_cparams("arbitrary"),
        name="mixer_out",
    )(x, o_a, o_b, proj_main, proj_main, proj_main, g1, sc2, sh2, gdn_norm, norm_ffn, wa, wb, wo)


def _swiglu_accumulate(h, wg_ref, wu_ref, wd_ref, acc):
    gte = jnp.dot(h, wg_ref[...], preferred_element_type=F32)
    up = jnp.dot(h, wu_ref[...], preferred_element_type=F32)
    acc[...] += jnp.dot((_silu(gte) * up).astype(BF16), wd_ref[...], preferred_element_type=F32)


def _ffn_body(h_ref, base_ref, g2_ref, wg_ref, wu_ref, wd_ref, o_ref, acc):
    j = pl.program_id(1)

    @pl.when(j == 0)
    def _():
        acc[...] = jnp.zeros_like(acc)

    _swiglu_accumulate(h_ref[...], wg_ref, wu_ref, wd_ref, acc)

    @pl.when(j == pl.num_programs(1) - 1)
    def _():
        o_ref[...] = base_ref[...] + g2_ref[...] * acc[...]


def _ffn(h2, base, g2, w_up, w_down, *, tm, tf, per_seq):
    rows, d = h2.shape
    ff = w_down.shape[0]
    nj = ff // tf
    tps = (rows // g2.shape[0]) // tm if per_seq else None
    return pl.pallas_call(
        _ffn_body,
        grid=(rows // tm, nj),
        in_specs=[pl.BlockSpec((tm, d), lambda i, j: (i, 0)),
                  pl.BlockSpec((tm, d), lambda i, j: (i, 0)),
                  _mod_spec(per_seq, tm, tps),
                  pl.BlockSpec((d, tf), lambda i, j: (0, j)),
                  pl.BlockSpec((d, tf), lambda i, j: (0, j + nj)),
                  pl.BlockSpec((tf, d), lambda i, j: (j, 0))],
        out_specs=pl.BlockSpec((tm, d), lambda i, j: (i, 0)),
        out_shape=jax.ShapeDtypeStruct((rows, d), F32),
        scratch_shapes=[pltpu.VMEM((tm, d), F32)],
        compiler_params=_cparams("arbitrary", "arbitrary"),
        name="ffn",
    )(h2, base, g2, w_up, w_up, w_down)


R_I1, R_I2, R_W1, R_W2, R_R1, R_R2 = range(6)


def _router_body(x_ref, sc_ref, sh_ref, nf_ref, wr_ref, route_ref, cnt_ref, carry):
    @pl.when(pl.program_id(0) == 0)
    def _():
        carry[...] = jnp.zeros_like(carry)

    h = _rms_mod(x_ref[...], nf_ref[...], sc_ref[...], sh_ref[...])
    logits = jnp.dot(h, wr_ref[...], preferred_element_type=F32, precision=lax.Precision.HIGHEST)
    lane = lax.broadcasted_iota(jnp.int32, logits.shape, 1).astype(F32)
    width = float(logits.shape[1])
    logits = jnp.where(lane < N_EXPERTS, logits, NEG)
    m1 = jnp.max(logits, axis=1, keepdims=True)
    i1 = jnp.min(jnp.where(logits == m1, lane, width), axis=1, keepdims=True)
    rest = jnp.where(lane == i1, NEG, logits)
    m2 = jnp.max(rest, axis=1, keepdims=True)
    i2 = jnp.min(jnp.where(rest == m2, lane, width), axis=1, keepdims=True)
    e2 = jnp.exp(m2 - m1)
    w1 = 1.0 / (1.0 + e2)
    sel = jnp.where(lane == i1, 1.0, 0.0) + jnp.where(lane == i2, 1.0, 0.0)
    incl = _cumsum_rows(sel) + carry[...]
    excl = incl - sel
    r1 = jnp.sum(jnp.where(lane == i1, excl, 0.0), axis=1, keepdims=True)
    r2 = jnp.sum(jnp.where(lane == i2, excl, 0.0), axis=1, keepdims=True)
    total = incl[incl.shape[0] - 1:, :]
    carry[...] = total
    cnt_ref[...] = total
    route = jnp.zeros_like(logits)
    for k, v in ((R_I1, i1), (R_I2, i2), (R_W1, w1), (R_W2, e2 * w1), (R_R1, r1), (R_R2, r2)):
        route = jnp.where(lane == k, v, route)
    route_ref[...] = route


def _router(x, sc2, sh2, norm_ffn, w_router_pad, *, tm, per_seq):
    rows, d = x.shape
    tps = (rows // sc2.shape[0]) // tm if per_seq else None
    ms = _mod_spec(per_seq, tm, tps)
    width = w_router_pad.shape[1]
    return pl.pallas_call(
        _router_body,
        grid=(rows // tm,),
        in_specs=[pl.BlockSpec((tm, d), lambda i: (i, 0)), ms, ms,
                  pl.BlockSpec((1, d), lambda i: (0, 0)),
                  pl.BlockSpec((d, width), lambda i: (0, 0))],
        out_specs=[pl.BlockSpec((tm, width), lambda i: (i, 0)), pl.BlockSpec((1, width), lambda i: (0, 0))],
        out_shape=[jax.ShapeDtypeStruct((rows, width), F32), jax.ShapeDtypeStruct((1, width), F32)],
        scratch_shapes=[pltpu.VMEM((1, width), F32)],
        compiler_params=_cparams("arbitrary"),
        name="router",
    )(x, sc2, sh2, norm_ffn, w_router_pad)


def _route_plan(route, counts, tm):
    rows = route.shape[0]
    n_tiles = -(-2 * rows // tm) + N_EXPERTS + 1
    i1 = route[:, R_I1].astype(jnp.int32)
    i2 = route[:, R_I2].astype(jnp.int32)
    cnt = counts[0, :N_EXPERTS].astype(jnp.int32)
    tiles = (cnt + tm - 1) // tm
    ends = jnp.cumsum(tiles)
    starts = (ends - tiles) * tm
    pos1 = starts[i1] + route[:, R_R1].astype(jnp.int32)
    pos2 = starts[i2] + route[:, R_R2].astype(jnp.int32)
    tile_expert = jnp.minimum(jnp.sum(jnp.arange(n_tiles)[:, None] >= ends[None, :], axis=1), N_EXPERTS - 1)
    tok = jnp.arange(rows, dtype=jnp.int32)
    row_token = jnp.zeros((n_tiles * tm,), jnp.int32).at[jnp.concatenate([pos1, pos2])].set(
        jnp.concatenate([tok, tok]), unique_indices=True, mode="promise_in_bounds")
    return dict(n_tiles=n_tiles, tile_expert=tile_expert.astype(jnp.int32), n_used=ends[-1:].astype(jnp.int32),
                row_token=row_token.reshape(n_tiles, 1, tm), pos1=pos1, pos2=pos2)


def _start_row_gather(idx_ref, n, src_hbm, dst, sem):
    def body(r, carry):
        src = pl.multiple_of(idx_ref[0, 0, r] * ROW_SLABS, ROW_SLABS)
        pltpu.make_async_copy(src_hbm.at[pl.ds(src, ROW_SLABS), :],
                              dst.at[pl.ds(pl.multiple_of(r * ROW_SLABS, ROW_SLABS), ROW_SLABS), :], sem).start()
        return carry
    lax.fori_loop(0, n, body, 0, unroll=8)


def _wait_row_gather(n, src_hbm, dst, sem):
    pltpu.make_async_copy(src_hbm.at[pl.ds(0, n * ROW_SLABS), :], dst, sem).wait()


def _moe_ffn_body(te_ref, nu_ref, rt_ref, rtn_ref, h_hbm, wg_ref, wu_ref, wd_ref, o_ref,
                  xbuf, hbuf, acc, sem, *, tm, nj):
    i = pl.program_id(0)
    j = pl.program_id(1)
    used = i < nu_ref[0]
    slot = i % 2
    per_step = tm // nj

    @pl.when(jnp.logical_and(j == 0, i == 0))
    def _():
        _start_row_gather(rt_ref, tm, h_hbm, xbuf.at[0], sem.at[0])

    @pl.when(jnp.logical_and(j == 0, i <= nu_ref[0]))
    def _():
        _wait_row_gather(tm, h_hbm, xbuf.at[slot], sem.at[slot])

    @pl.when(jnp.logical_and(j == 0, used))
    def _():
        for s in range(ROW_SLABS):
            hbuf[:, s * LANES:(s + 1) * LANES] = xbuf[slot, pl.ds(s, tm, stride=ROW_SLABS), :].astype(BF16)
        acc[...] = jnp.zeros_like(acc)

    @pl.when(used)
    def _():
        for k in range(per_step):
            r = j * per_step + k
            src = pl.multiple_of(rtn_ref[0, 0, r] * ROW_SLABS, ROW_SLABS)
            pltpu.make_async_copy(
                h_hbm.at[pl.ds(src, ROW_SLABS), :],
                xbuf.at[1 - slot, pl.ds(pl.multiple_of(r * ROW_SLABS, ROW_SLABS), ROW_SLABS), :],
                sem.at[1 - slot]).start()
        _swiglu_accumulate(hbuf[...], wg_ref, wu_ref, wd_ref, acc)

    last = j == pl.num_programs(1) - 1

    @pl.when(jnp.logical_and(last, used))
    def _():
        for s in range(ROW_SLABS):
            o_ref[pl.ds(s, tm, stride=ROW_SLABS), :] = acc[:, s * LANES:(s + 1) * LANES]

    @pl.when(jnp.logical_and(last, jnp.logical_not(used)))
    def _():
        o_ref[...] = jnp.zeros_like(o_ref)


def _moe_ffn(h_rows, plan, w_up, w_down, moe_idx, *, tm, tf):
    d = w_up.shape[-2]
    ff = w_down.shape[-2]
    nj = ff // tf
    n_tiles = plan["n_tiles"]

    def jj(i, j, nu):
        return jnp.where(i < nu[0], j, nj - 1)

    grid_spec = pltpu.PrefetchScalarGridSpec(
        num_scalar_prefetch=2,
        grid=(n_tiles, nj),
        in_specs=[pl.BlockSpec((1, 1, tm), lambda i, j, te, nu: (i, 0, 0), memory_space=pltpu.SMEM),
                  pl.BlockSpec((1, 1, tm), lambda i, j, te, nu: (jnp.minimum(i + 1, n_tiles - 1), 0, 0),
                               memory_space=pltpu.SMEM),
                  pl.BlockSpec(memory_space=pl.ANY),
                  pl.BlockSpec((None, None, d, tf), lambda i, j, te, nu: (moe_idx, te[i], 0, jj(i, j, nu))),
                  pl.BlockSpec((None, None, d, tf), lambda i, j, te, nu: (moe_idx, te[i], 0, jj(i, j, nu) + nj)),
                  pl.BlockSpec((None, None, tf, d), lambda i, j, te, nu: (moe_idx, te[i], jj(i, j, nu), 0))],
        out_specs=pl.BlockSpec((tm * ROW_SLABS, LANES), lambda i, j, te, nu: (i, 0)),
        scratch_shapes=[pltpu.VMEM((2, tm * ROW_SLABS, LANES), F32), pltpu.VMEM((tm, d), BF16),
                        pltpu.VMEM((tm, d), F32), pltpu.SemaphoreType.DMA((2,))])
    return pl.pallas_call(
        functools.partial(_moe_ffn_body, tm=tm, nj=nj),
        grid_spec=grid_spec,
        out_shape=jax.ShapeDtypeStruct((n_tiles * tm * ROW_SLABS, LANES), F32),
        compiler_params=pltpu.CompilerParams(dimension_semantics=("arbitrary", "arbitrary"),
                                             vmem_limit_bytes=VMEM_LIMIT, disable_bounds_checks=True),
        name="moe_ffn",
    )(plan["tile_expert"], plan["n_used"], plan["row_token"], plan["row_token"], h_rows, w_up, w_up, w_down)


def _moe_combine_body(p1_ref, p2_ref, p1n_ref, p2n_ref, y_hbm, x_ref, g2_ref, route_ref, fw_ref, o_ref,
                      buf1, buf2, sem, *, tc, final):
    i = pl.program_id(0)
    slot = i % 2

    @pl.when(i == 0)
    def _():
        _start_row_gather(p1_ref, tc, y_hbm, buf1.at[0], sem.at[0, 0])
        _start_row_gather(p2_ref, tc, y_hbm, buf2.at[0], sem.at[1, 0])

    _wait_row_gather(tc, y_hbm, buf1.at[slot], sem.at[0, slot])
    _wait_row_gather(tc, y_hbm, buf2.at[slot], sem.at[1, slot])

    @pl.when(i + 1 < pl.num_programs(0))
    def _():
        _start_row_gather(p1n_ref, tc, y_hbm, buf1.at[1 - slot], sem.at[0, 1 - slot])
        _start_row_gather(p2n_ref, tc, y_hbm, buf2.at[1 - slot], sem.at[1, 1 - slot])

    w1 = route_ref[:, R_W1:R_W1 + 1]
    w2 = route_ref[:, R_W2:R_W2 + 1]
    pieces = []
    for s in range(ROW_SLABS):
        cols = slice(s * LANES, (s + 1) * LANES)
        y = (w1 * buf1[slot, pl.ds(s, tc, stride=ROW_SLABS), :]
             + w2 * buf2[slot, pl.ds(s, tc, stride=ROW_SLABS), :])
        pieces.append(x_ref[:, cols] + g2_ref[:, cols] * y)
    xn = jnp.concatenate(pieces, axis=1)
    if final:
        xn = xn * lax.rsqrt(jnp.mean(xn * xn, axis=-1, keepdims=True) + NORM_EPS) * fw_ref[...]
    o_ref[...] = xn


def _moe_combine(y_rows, plan, route, x, g2, final_w, *, tc, per_seq, final):
    rows, d = x.shape
    nt = rows // tc
    tps = (rows // g2.shape[0]) // tc if per_seq else None
    p1 = plan["pos1"].reshape(nt, 1, tc)
    p2 = plan["pos2"].reshape(nt, 1, tc)
    cur = pl.BlockSpec((1, 1, tc), lambda i: (i, 0, 0), memory_space=pltpu.SMEM)
    nxt = pl.BlockSpec((1, 1, tc), lambda i: (jnp.minimum(i + 1, nt - 1), 0, 0), memory_space=pltpu.SMEM)
    return pl.pallas_call(
        functools.partial(_moe_combine_body, tc=tc, final=final),
        grid=(nt,),
        in_specs=[cur, cur, nxt, nxt, pl.BlockSpec(memory_space=pl.ANY),
                  pl.BlockSpec((tc, d), lambda i: (i, 0)), _mod_spec(per_seq, tc, tps),
                  pl.BlockSpec((tc, route.shape[1]), lambda i: (i, 0)),
                  pl.BlockSpec((1, d), lambda i: (0, 0))],
        out_specs=pl.BlockSpec((tc, d), lambda i: (i, 0)),
        out_shape=jax.ShapeDtypeStruct((rows, d), F32),
        scratch_shapes=[pltpu.VMEM((2, tc * ROW_SLABS, LANES), F32), pltpu.VMEM((2, tc * ROW_SLABS, LANES), F32),
                        pltpu.SemaphoreType.DMA((2, 2))],
        compiler_params=pltpu.CompilerParams(dimension_semantics=("arbitrary",),
                                             vmem_limit_bytes=VMEM_LIMIT, disable_bounds_checks=True),
        name="moe_combine",
    )(p1, p2, p1, p2, y_rows, x, g2, route, final_w)


def _final_body(x_ref, w_ref, o_ref):
    x = x_ref[...]
    o_ref[...] = x * lax.rsqrt(jnp.mean(x * x, axis=-1, keepdims=True) + NORM_EPS) * w_ref[...]


def _final_norm(x, w, *, tm):
    rows, d = x.shape
    return pl.pallas_call(
        _final_body,
        grid=(rows // tm,),
        in_specs=[pl.BlockSpec((tm, d), lambda i: (i, 0)), pl.BlockSpec((1, d), lambda i: (0, 0))],
        out_specs=pl.BlockSpec((tm, d), lambda i: (i, 0)),
        out_shape=jax.ShapeDtypeStruct((rows, d), F32),
        compiler_params=_cparams("arbitrary"),
        name="final_norm",
    )(x, w)


def _rope_tables(pos):
    half = A_HEAD_DIM // 2
    inv = jnp.exp(-math.log(ROPE_THETA) * jnp.arange(half, dtype=F32) / half)
    ang = pos.astype(F32)[:, None] * inv[None, :]
    cos, sin = jnp.cos(ang), jnp.sin(ang)
    cos_t = jnp.tile(jnp.concatenate([cos, cos], axis=1), (1, HEADS_PER_SLAB))
    sin_t = jnp.tile(jnp.concatenate([-sin, sin], axis=1), (1, HEADS_PER_SLAB))
    return cos_t, sin_t


def _slab_rows(pa, c, nseq, seq_len, keep):
    if keep == seq_len:
        t = pa[c * SLABS:(c + 1) * SLABS]
    else:
        t = jnp.stack([lax.slice(pa, (c * SLABS, n * seq_len + seq_len - keep, 0),
                                 ((c + 1) * SLABS, (n + 1) * seq_len, LANES)) for n in range(nseq)], axis=1)
    t = t.reshape(SLABS, nseq, keep, HEADS_PER_SLAB, A_HEAD_DIM)
    return t.transpose(1, 2, 0, 3, 4).reshape(nseq, keep, A_HEADS, A_HEAD_DIM)


def _split_w_in(w):
    o_b = A_QKV
    o_z = o_b + CONV_CH
    o_ba = o_z + B_WIDTH
    o_g = o_ba + 2 * B_HEADS
    pad = jnp.zeros((w.shape[0], BA_PAD - 2 * B_HEADS), w.dtype)
    main = jnp.concatenate([w[:, o_b:o_ba], w[:, o_g:], w[:, o_ba:o_g], pad], axis=1)
    return w[:, :A_QKV].astype(BF16), main.astype(BF16)


def kernel(x_prompt, x_sample, cache_k_g0, cache_v_g0, cache_k_g1, cache_v_g1, cache_k_g2, cache_v_g2,
           state_conv, state_delta, c_prompt, c_sample, w_mod, b_mod, norm_mix, norm_ffn, w_in, conv_w,
           a_log, dt_bias, gdn_norm, w_branch_a, w_branch_b, w_out, w_ffn_up, w_ffn_down, w_router,
           w_exp_up, w_exp_down, final_norm):
    depth = w_in.shape[0]
    nb, seq, d = x_prompt.shape
    ns, dec, _ = x_sample.shape
    past = cache_k_g2.shape[2]

    c_all = jnp.concatenate([c_prompt, c_sample, jnp.zeros((-(nb + ns) % 8, d), F32)], axis=0)
    mod_all = _modulation(c_all, w_mod, b_mod)

    caches = [_window_minor(c) for c in (cache_k_g0, cache_v_g0, cache_k_g1, cache_v_g1, cache_k_g2, cache_v_g2)]
    tabs_p = _rope_tables(jnp.tile(jnp.arange(seq), nb))
    tabs_s = _rope_tables(jnp.tile(past + jnp.arange(dec), ns))
    zero_conv = jnp.zeros((depth, nb, CONV_W - 1, CONV_CH), F32)
    zero_state = jnp.zeros((depth, nb, B_HEADS, B_HEAD_DIM, B_HEAD_DIM), F32)

    exp_up = w_exp_up.astype(BF16)
    exp_down = w_exp_down.astype(BF16)
    xp = x_prompt.reshape(nb * seq, d)
    xs = x_sample.reshape(ns * dec, d)
    new_p = [[] for _ in range(8)]
    new_s = [[] for _ in range(8)]
    tm_p, tm_s = 512, 512

    for l in range(depth):
        mods_p = [m.reshape(nb, 1, d) for m in jnp.split(mod_all[l, :nb], 6, axis=-1)]
        mods_s = [jnp.repeat(m, dec, axis=0) for m in jnp.split(mod_all[l, nb:nb + ns], 6, axis=-1)]
        w_attn, w_main = _split_w_in(w_in[l])
        nw_mix = norm_mix[l].reshape(1, d)
        nw_ffn = norm_ffn[l].reshape(1, d)
        a_par = jnp.zeros((2, BA_PAD), F32)
        a_par = a_par.at[0, B_HEADS:2 * B_HEADS].set(a_log[l]).at[1, B_HEADS:2 * B_HEADS].set(dt_bias[l])
        wa = w_branch_a[l].astype(BF16)
        wb = w_branch_b[l].astype(BF16)
        wo = w_out[l].astype(BF16)
        gn = gdn_norm[l].reshape(1, B_HEAD_DIM)

        outs = []
        for (x, mods, per_seq, tm, tabs, is_prompt) in ((xp, mods_p, True, tm_p, tabs_p, True),
                                                        (xs, mods_s, False, tm_s, tabs_s, False)):
            sh1, sc1, g1, sh2, sc2, g2 = mods
            tm_proj = min(x.shape[0], PROJ_ROWS)
            pm = _project(x, sc1, sh1, nw_mix, w_main, tm=tm_proj, tn=MAIN_COLS // 5, per_seq=per_seq,
                          name="proj_main")
            pa = _project(x, sc1, sh1, nw_mix, w_attn, tm=tm_proj, tn=3 * A_WIDTH, per_seq=per_seq,
                          rope_tabs=tabs, name="proj_attn")
            if is_prompt:
                run = None
                for gi in range(N_GROUPS):
                    run = _band_attention(pa, gi, seq, run)
                o_a = run[0]
                o_b, s_new = _delta_rule(pm, zero_conv, zero_state, l, conv_w[l], a_par, seq, PROMPT_CHUNK, nb)
            else:
                o_a = _cached_attention(pa, caches, l, dec)
                o_b, s_new = _delta_rule(pm, state_conv, state_delta, l, conv_w[l], a_par, dec, dec, SAMPLE_SEQS_PER_STEP)
            x, h2 = _mixer_out(x, o_a, o_b, pm, g1, sc2, sh2, gn, nw_ffn, wa, wb, wo, tm=tm, per_seq=per_seq,
                               token_major=l % 2 == 1)
            if l % 2 == 0:
                x = _ffn(h2, x, g2, w_ffn_up[l // 2].astype(BF16), w_ffn_down[l // 2].astype(BF16),
                         tm=tm, tf=1408, per_seq=per_seq)
            else:
                wr = jnp.zeros((d, LANES), F32).at[:, :N_EXPERTS].set(w_router[l // 2])
                route, counts = _router(x, sc2, sh2, nw_ffn, wr, tm=tm, per_seq=per_seq)
                tm_moe = MOE_ROWS if x.shape[0] >= 8 * MOE_ROWS else MOE_ROWS // 2
                plan = _route_plan(route, counts, tm_moe)
                y_rows = _moe_ffn(h2, plan, exp_up, exp_down, l // 2, tm=tm_moe, tf=1792)
                x = _moe_combine(y_rows, plan, route, x, g2, final_norm.reshape(1, d), tc=min(512, x.shape[0]),
                                 per_seq=per_seq, final=l == depth - 1)
            outs.append((x, pa, pm, s_new))

        (xp, pa_p, pm_p, s_p), (xs, pa_s, pm_s, s_s) = outs
        for gi, (win, dil) in enumerate(ATTN_GROUPS):
            keep = min(win, seq)
            new_p[2 * gi].append(_slab_rows(pa_p, 3 * gi + 1, nb, seq, keep))
            new_p[2 * gi + 1].append(_slab_rows(pa_p, 3 * gi + 2, nb, seq, keep))
            new_s[2 * gi].append(_slab_rows(pa_s, 3 * gi + 1, ns, dec, dec))
            new_s[2 * gi + 1].append(_slab_rows(pa_s, 3 * gi + 2, ns, dec, dec))
        new_p[6].append(pm_p.reshape(nb, seq, MAIN_COLS)[:, seq - (CONV_W - 1):, :CONV_CH])
        new_p[7].append(s_p)
        conv_s = jnp.concatenate([state_conv[l], pm_s.reshape(ns, dec, MAIN_COLS)[:, :, :CONV_CH]], axis=1)
        new_s[6].append(conv_s[:, -(CONV_W - 1):])
        new_s[7].append(s_s)

    if depth % 2 == 1:
        xp = _final_norm(xp, final_norm.reshape(1, d), tm=tm_p)
        xs = _final_norm(xs, final_norm.reshape(1, d), tm=tm_s)
    y_prompt = xp.reshape(nb, seq, d)
    y_sample = xs.reshape(ns, dec, d)
    return (y_prompt, y_sample, *[jnp.stack(t) for t in new_p], *[jnp.stack(t) for t in new_s])
```

```python
import functools
import math

import numpy as np
import jax
import jax.numpy as jnp
from jax import lax
from jax.experimental import pallas as pl
from jax.experimental.pallas import tpu as pltpu

F32 = jnp.float32
BF16 = jnp.bfloat16

D_MODEL = 1024
ATTN_GROUPS = ((128, 1), (512, 4), (2048, 16))
N_GROUPS = 3
A_HEADS = 4
A_HEAD_DIM = 64
A_WIDTH = A_HEADS * A_HEAD_DIM
LANES = 128
SLABS = A_WIDTH // LANES
HEADS_PER_SLAB = LANES // A_HEAD_DIM
ROW_SLABS = D_MODEL // LANES
ROPE_THETA = 10000.0
BAND_BLOCK = 128
CACHED_SEQS_PER_STEP = 2
BAND_TILING = ((8, 1), (4, 1), (1, 4))
B_HEADS = 8
B_HEAD_DIM = 128
B_WIDTH = B_HEADS * B_HEAD_DIM
CONV_W = 4
CONV_CH = 3 * B_WIDTH
PROMPT_CHUNK = 128
SAMPLE_SEQS_PER_STEP = 4
PROJ_ROWS = 1024
MOE_ROWS = 512
N_EXPERTS = 8
NORM_EPS = 1e-6
A_QKV = N_GROUPS * 3 * A_WIDTH
NEG = -1e30

BA_PAD = 256
MAIN_COLS = CONV_CH + 3 * D_MODEL + BA_PAD
COL_Z = CONV_CH // D_MODEL
COL_GA = COL_Z + 1
COL_GB = COL_Z + 2
COL_BA = (CONV_CH + 3 * D_MODEL) // BA_PAD

VMEM_LIMIT = 52 * 1024 * 1024


def _cparams(*sem):
    return pltpu.CompilerParams(dimension_semantics=sem, vmem_limit_bytes=VMEM_LIMIT)


def _sigmoid(x):
    return 1.0 / (1.0 + jnp.exp(-x))


def _silu(x):
    return x * _sigmoid(x)


def _nt_dot(a, b):
    return lax.dot_general(a, b, (((1,), (1,)), ((), ())), preferred_element_type=F32)


def _tn_dot(a, b):
    return lax.dot_general(a, b, (((0,), (0,)), ((), ())), preferred_element_type=F32)


def _mod_spec(per_seq, tm, tiles_per_seq):
    if per_seq:
        return pl.BlockSpec((None, 1, D_MODEL), lambda i, *_: (i // tiles_per_seq, 0, 0))
    return pl.BlockSpec((tm, D_MODEL), lambda i, *_: (i, 0))


def _rms_mod(x, nw, sc, sh):
    y = x * lax.rsqrt(jnp.mean(x * x, axis=-1, keepdims=True) + NORM_EPS) * nw
    return y * (1.0 + sc) + sh


def _mod_body(c_ref, w_ref, b_ref, o_ref):
    s = _silu(c_ref[...]).astype(BF16)
    o_ref[...] = jnp.dot(s, w_ref[...].astype(BF16), preferred_element_type=F32) + b_ref[...]


def _modulation(c_all, w_mod, b_mod):
    depth, d, n = w_mod.shape
    rows = c_all.shape[0]
    tn = 1536
    return pl.pallas_call(
        _mod_body,
        grid=(depth, n // tn),
        in_specs=[pl.BlockSpec((rows, d), lambda l, j: (0, 0)),
                  pl.BlockSpec((None, d, tn), lambda l, j: (l, 0, j)),
                  pl.BlockSpec((None, 1, tn), lambda l, j: (l, 0, j))],
        out_specs=pl.BlockSpec((None, rows, tn), lambda l, j: (l, 0, j)),
        out_shape=jax.ShapeDtypeStruct((depth, rows, n), F32),
        compiler_params=_cparams("arbitrary", "arbitrary"),
        name="adaln_mod",
    )(c_all, w_mod, b_mod.reshape(depth, 1, n))


def _proj_body(x_ref, sc_ref, sh_ref, nw_ref, w_ref, *rest, rope):
    if rope:
        cos_ref, sin_ref, o_ref, h_scr = rest
    else:
        o_ref, h_scr = rest

    @pl.when(pl.program_id(1) == 0)
    def _():
        h_scr[...] = _rms_mod(x_ref[...], nw_ref[...], sc_ref[...], sh_ref[...]).astype(BF16)

    y = jnp.dot(h_scr[...], w_ref[...], preferred_element_type=F32)
    if not rope:
        o_ref[...] = y
        return
    cos = cos_ref[...]
    sin = sin_ref[...]
    lane = lax.broadcasted_iota(jnp.int32, cos.shape, 1)
    first_half = (lane % A_HEAD_DIM) < (A_HEAD_DIM // 2)
    for c in range(3 * SLABS):
        t = y[:, c * LANES:(c + 1) * LANES]
        if c < 2 * SLABS:
            partner = jnp.where(first_half, pltpu.roll(t, LANES - A_HEAD_DIM // 2, 1),
                                pltpu.roll(t, A_HEAD_DIM // 2, 1))
            t = t * cos + partner * sin
        o_ref[c] = t


def _project(x, sc, sh, nw, w, *, tm, tn, per_seq, rope_tabs=None, name):
    rows, d = x.shape
    n = w.shape[1]
    tps = (rows // sc.shape[0]) // tm if per_seq else None
    in_specs = [pl.BlockSpec((tm, d), lambda i, j: (i, 0)),
                _mod_spec(per_seq, tm, tps), _mod_spec(per_seq, tm, tps),
                pl.BlockSpec((1, d), lambda i, j: (0, 0)),
                pl.BlockSpec((d, tn), lambda i, j: (0, j))]
    args = [x, sc, sh, nw, w]
    out_spec = pl.BlockSpec((tm, tn), lambda i, j: (i, j))
    out_shape = jax.ShapeDtypeStruct((rows, n), F32)
    if rope_tabs is not None:
        in_specs += [pl.BlockSpec((tm, LANES), lambda i, j: (i, 0))] * 2
        args += list(rope_tabs)
        out_spec = pl.BlockSpec((tn // LANES, tm, LANES), lambda i, j: (j, i, 0))
        out_shape = jax.ShapeDtypeStruct((n // LANES, rows, LANES), F32)
    return pl.pallas_call(
        functools.partial(_proj_body, rope=rope_tabs is not None),
        grid=(rows // tm, n // tn),
        in_specs=in_specs,
        out_specs=out_spec,
        out_shape=out_shape,
        scratch_shapes=[pltpu.VMEM((tm, d), BF16)],
        compiler_params=_cparams("arbitrary", "arbitrary"),
        name=name,
    )(*args)


def _head_masks(width):
    lane = lax.broadcasted_iota(jnp.int32, (1, width), 1)
    return [(lane // A_HEAD_DIM == h).astype(F32) for h in range(A_HEADS)]


def _merge_lse(o_a, l_a, o_b, l_b):
    m = jnp.maximum(l_a, l_b)
    e_a = jnp.exp(l_a - m)
    e_b = jnp.exp(l_b - m)
    den = e_a + e_b
    return (o_a * e_a + o_b * e_b) / den, m + jnp.log(den)


def _band_body(q_ref, kc_ref, kp_ref, vc_ref, vp_ref, bias_ref, *rest, dil, nb, res_per_iter, has_prev_group):
    if has_prev_group:
        oprev_ref, lprev_ref, o_ref, l_ref = rest
    else:
        o_ref, l_ref = rest
    blk = BAND_BLOCK
    col = lax.broadcasted_iota(jnp.int32, (blk, 2 * blk), 1)
    no_prev = jnp.logical_and(pl.program_id(1) == 0, col < blk)
    bias_h = jnp.concatenate([bias_ref[...]] * HEADS_PER_SLAB, axis=0)
    bias_h0 = jnp.concatenate([jnp.where(no_prev, NEG, bias_ref[...])] * HEADS_PER_SLAB, axis=0)
    hm = _head_masks(LANES)

    def rows_of(r, b):
        return pl.ds(r + dil * blk * b, blk, stride=dil) if dil > 1 else pl.ds(blk * b, blk)

    def with_prev(cur_ref, prev_ref, sl, r, b):
        prev = prev_ref[sl, rows_of(r, 0), :] if b == 0 else cur_ref[sl, rows_of(r, b - 1), :]
        return jnp.concatenate([prev, cur_ref[sl, rows_of(r, b), :]], axis=0).astype(BF16)

    def process(residues):
        units = [dict(r=r, b=b, sl=sl) for r in residues for b in range(nb) for sl in range(SLABS)]
        for un in units:
            q = q_ref[un["sl"], rows_of(un["r"], un["b"]), :] * (A_HEAD_DIM ** -0.5)
            qh = jnp.concatenate([q * hm[h] for h in range(HEADS_PER_SLAB)], axis=0).astype(BF16)
            k = with_prev(kc_ref, kp_ref, un["sl"], un["r"], un["b"])
            un["s"] = _nt_dot(qh, k) + (bias_h0 if un["b"] == 0 else bias_h)
        for un in units:
            m = jnp.max(un["s"], axis=1, keepdims=True)
            p = jnp.exp(un["s"] - m)
            un["l"] = jnp.sum(p, axis=1, keepdims=True)
            un["m"] = m
            un["p"] = p.astype(BF16)
        for un in units:
            v = with_prev(vc_ref, vp_ref, un["sl"], un["r"], un["b"])
            un["oh"] = jnp.dot(un["p"], v, preferred_element_type=F32)
        for un in units:
            rows = rows_of(un["r"], un["b"])
            oh = un["oh"] / un["l"]
            lseh = un["m"] + jnp.log(un["l"])
            o = sum(oh[h * blk:(h + 1) * blk] * hm[h] for h in range(HEADS_PER_SLAB))
            lse = sum(lseh[h * blk:(h + 1) * blk] * hm[h] for h in range(HEADS_PER_SLAB))
            if has_prev_group:
                o, lse = _merge_lse(oprev_ref[un["sl"], rows, :], lprev_ref[un["sl"], rows, :], o, lse)
            o_ref[un["sl"], rows, :] = o
            l_ref[un["sl"], rows, :] = lse

    if dil <= res_per_iter:
        process(list(range(dil)))
    else:
        def loop_body(it, carry):
            process([it * res_per_iter + k for k in range(res_per_iter)])
            return carry
        lax.fori_loop(0, dil // res_per_iter, loop_body, 0)


def _band_bias(nk):
    qi = np.arange(BAND_BLOCK)[:, None]
    kj = np.arange(2 * BAND_BLOCK)[None, :]
    rel = kj - BAND_BLOCK - qi
    return jnp.asarray(np.where((rel <= 0) & (rel >= -nk), 0.0, NEG), F32)


def _band_attention(qkv, gi, seq_len, prev):
    win, dil = ATTN_GROUPS[gi]
    rows = qkv.shape[1]
    nb, res_per_iter = BAND_TILING[gi]
    blk_rows = BAND_BLOCK * dil
    nb = min(nb, seq_len // blk_rows)
    sb_rows = blk_rows * nb
    nsb = seq_len // sb_rows
    nseq = rows // seq_len

    def cur(c):
        return pl.BlockSpec((SLABS, sb_rows, LANES), lambda n, s: (c, n * nsb + s, 0))

    def prv(c):
        return pl.BlockSpec((SLABS, blk_rows, LANES),
                            lambda n, s: (c, jnp.maximum((n * nsb + s) * nb - 1, n * nsb * nb), 0))

    row_spec = pl.BlockSpec((SLABS, sb_rows, LANES), lambda n, s: (0, n * nsb + s, 0))
    in_specs = [cur(3 * gi), cur(3 * gi + 1), prv(3 * gi + 1), cur(3 * gi + 2), prv(3 * gi + 2),
                pl.BlockSpec((BAND_BLOCK, 2 * BAND_BLOCK), lambda n, s: (0, 0))]
    args = [qkv, qkv, qkv, qkv, qkv, _band_bias(win // dil)]
    if prev is not None:
        in_specs += [row_spec, row_spec]
        args += list(prev)
    return pl.pallas_call(
        functools.partial(_band_body, dil=dil, nb=nb, res_per_iter=res_per_iter, has_prev_group=prev is not None),
        grid=(nseq, nsb),
        in_specs=in_specs,
        out_specs=[row_spec, row_spec],
        out_shape=[jax.ShapeDtypeStruct((SLABS, rows, LANES), F32)] * 2,
        compiler_params=_cparams("arbitrary", "arbitrary"),
        name="band_attn_g%d" % gi,
    )(*args)


def _cached_body(row_ref, *rest, n_new, nsq):
    caches = rest[:2 * N_GROUPS]
    biases = rest[2 * N_GROUPS:4 * N_GROUPS]
    o_ref = rest[4 * N_GROUPS]
    hm = _head_masks(LANES)
    units = [dict(sq=sq, sl=sl, gi=gi) for sq in range(nsq) for sl in range(SLABS) for gi in range(N_GROUPS)]
    for un in units:
        sl, gi = un["sl"], un["gi"]
        un["rows"] = slice(un["sq"] * n_new, (un["sq"] + 1) * n_new)
        q = row_ref[(3 * gi) * SLABS + sl, un["rows"], :] * (A_HEAD_DIM ** -0.5)
        kn = row_ref[(3 * gi + 1) * SLABS + sl, un["rows"], :].astype(BF16)
        qh = jnp.concatenate([q * hm[h] for h in range(HEADS_PER_SLAB)], axis=0).astype(BF16)
        kct = caches[2 * gi][un["sq"], sl].astype(BF16)
        un["s_c"] = jnp.dot(qh, kct, preferred_element_type=F32) + biases[2 * gi][...]
        un["s_n"] = _nt_dot(qh, kn) + biases[2 * gi + 1][...]
    for un in units:
        m = jnp.maximum(jnp.max(un["s_c"], axis=1, keepdims=True), jnp.max(un["s_n"], axis=1, keepdims=True))
        p_c = jnp.exp(un["s_c"] - m)
        p_n = jnp.exp(un["s_n"] - m)
        un["l"] = jnp.sum(p_c, axis=1, keepdims=True) + jnp.sum(p_n, axis=1, keepdims=True)
        un["m"] = m
        un["p_c"] = p_c.astype(BF16)
        un["p_n"] = p_n.astype(BF16)
    for un in units:
        sl, gi = un["sl"], un["gi"]
        vn = row_ref[(3 * gi + 2) * SLABS + sl, un["rows"], :].astype(BF16)
        vct = caches[2 * gi + 1][un["sq"], sl].astype(BF16)
        un["oh"] = _nt_dot(un["p_c"], vct) + jnp.dot(un["p_n"], vn, preferred_element_type=F32)
    for first in range(0, len(units), N_GROUPS):
        o_run = l_run = None
        for un in units[first:first + N_GROUPS]:
            oh = un["oh"] / un["l"]
            lseh = un["m"] + jnp.log(un["l"])
            o = sum(oh[h * n_new:(h + 1) * n_new] * hm[h] for h in range(HEADS_PER_SLAB))
            lse = sum(lseh[h * n_new:(h + 1) * n_new] * hm[h] for h in range(HEADS_PER_SLAB))
            if un["gi"] == 0:
                o_run, l_run = o, lse
            else:
                o_run, l_run = _merge_lse(o_run, l_run, o, lse)
        o_ref[un["sl"], un["rows"], :] = o_run


def _cached_bias(wb, n_new, win, dil):
    nk = win // dil
    t = np.tile(np.arange(n_new), HEADS_PER_SLAB)[:, None]
    delta = wb + t - np.arange(wb + n_new)[None, :]
    ok = (delta >= 0) & (delta % dil == 0) & (delta <= dil * nk)
    b = np.where(ok, 0.0, NEG).astype(np.float32)
    return jnp.asarray(b[:, :wb]), jnp.asarray(b[:, wb:])


def _window_minor(c):
    depth, n, wb = c.shape[:3]
    return c.transpose(0, 1, 3, 4, 2).reshape(depth, n, SLABS, LANES, wb)


def _cached_attention(qkv, caches, layer, n_new):
    nslab, rows, _ = qkv.shape
    nseq = rows // n_new
    nsq = CACHED_SEQS_PER_STEP if nseq % CACHED_SEQS_PER_STEP == 0 else 1
    in_specs = [pl.BlockSpec((nslab, nsq * n_new, LANES), lambda n: (0, n, 0))]
    args = [qkv]
    for c in caches:
        in_specs.append(pl.BlockSpec((None, nsq, SLABS, LANES, c.shape[4]), lambda n: (layer, n, 0, 0, 0)))
        args.append(c)
    for gi, (win, dil) in enumerate(ATTN_GROUPS):
        for b in _cached_bias(caches[2 * gi].shape[4], n_new, win, dil):
            in_specs.append(pl.BlockSpec(b.shape, lambda n: (0, 0)))
            args.append(b)
    return pl.pallas_call(
        functools.partial(_cached_body, n_new=n_new, nsq=nsq),
        grid=(nseq // nsq,),
        in_specs=in_specs,
        out_specs=pl.BlockSpec((SLABS, nsq * n_new, LANES), lambda n: (0, n, 0)),
        out_shape=jax.ShapeDtypeStruct((SLABS, rows, LANES), F32),
        compiler_params=_cparams("arbitrary"),
        name="cached_attn",
    )(*args)


def _cumsum_rows(x):
    n = x.shape[0]
    row = lax.broadcasted_iota(jnp.int32, x.shape, 0)
    sh = 1
    while sh < n:
        x = x + jnp.where(row >= sh, pltpu.roll(x, sh, 0), 0.0)
        sh *= 2
    return x


def _delta_body(u_ref, ba_ref, conv0_ref, s0_ref, cw_ref, ap_ref, o_ref, s_ref, ubuf, *, chunk, nsb):
    C = chunk
    f32dot = functools.partial(jnp.dot, preferred_element_type=F32)

    @pl.when(pl.program_id(1) == 0)
    def _():
        ubuf[:, 0:8, :] = jnp.zeros((nsb, 8, CONV_CH), F32)
        ubuf[:, 8 - (CONV_W - 1):8, :] = conv0_ref[...]
        s_ref[...] = s0_ref[...]

    ubuf[:, 8:8 + C, :] = u_ref[...]
    cw = cw_ref[...]
    ap = ap_ref[...]
    ri = lax.broadcasted_iota(jnp.int32, (C, C), 0)
    ci = lax.broadcasted_iota(jnp.int32, (C, C), 1)
    eye_f = (ri == ci).astype(F32)
    levels = []
    b = 1
    while b < C:
        levels.append(((ri // (2 * b)) == (ci // (2 * b))) & ((ri // b) % 2 == 1) & ((ci // b) % 2 == 0))
        b *= 2

    units = []
    for sq in range(nsb):
        conv = ubuf[sq, 8:8 + C, :] * cw[CONV_W - 1:CONV_W]
        for i in range(CONV_W - 1):
            off = 8 - (CONV_W - 1) + i
            conv = conv + ubuf[sq, off:off + C, :] * cw[i:i + 1]
        cu = _silu(conv)
        ba = ba_ref[sq]
        beta_f = _sigmoid(ba)
        z = ba + ap[1:2]
        softplus = jnp.maximum(z, 0.0) + jnp.log(1.0 + jnp.exp(-jnp.abs(z)))
        g_cum = _cumsum_rows(-jnp.exp(ap[0:1]) * softplus)
        g_rows = g_cum[:, 0:LANES].T
        for h in range(B_HEADS):
            lo = h * B_HEAD_DIM
            qh = cu[:, lo:lo + B_HEAD_DIM]
            kh = cu[:, B_WIDTH + lo:B_WIDTH + lo + B_HEAD_DIM]
            qn = qh * lax.rsqrt(jnp.sum(qh * qh, axis=-1, keepdims=True) + NORM_EPS) * (B_HEAD_DIM ** -0.5)
            kn = kh * lax.rsqrt(jnp.sum(kh * kh, axis=-1, keepdims=True) + NORM_EPS)
            g_col = g_cum[:, B_HEADS + h:B_HEADS + h + 1]
            units.append(dict(sq=sq, h=h, qn=qn, kn=kn, kb=kn.astype(BF16),
                              vh=cu[:, 2 * B_WIDTH + lo:2 * B_WIDTH + lo + B_HEAD_DIM],
                              beta=beta_f[:, h:h + 1], g_col=g_col,
                              g_row=g_rows[B_HEADS + h:B_HEADS + h + 1, :]))
    ubuf[:, 0:8, :] = ubuf[:, C:C + 8, :]

    for un in units:
        un["qkk"] = _nt_dot(jnp.concatenate([un["kn"], un["qn"]], axis=0).astype(BF16), un["kb"])
    for un in units:
        dm = jnp.where(ri >= ci, jnp.exp(jnp.where(ri >= ci, un["g_col"] - un["g_row"], 0.0)), 0.0)
        a = jnp.where(ri > ci, un["beta"] * un["qkk"][0:C] * dm, 0.0)
        un["qk"] = (un["qkk"][C:2 * C] * dm).astype(BF16)
        un["ab"] = a.astype(BF16)
        un["t"] = eye_f - jnp.where(levels[0], a, 0.0)
    for lvl in levels[1:]:
        for un in units:
            un["tb"] = un["t"].astype(BF16)
            un["ta"] = f32dot(un["tb"], un["ab"]).astype(BF16)
        for un in units:
            un["t"] = un["t"] - jnp.where(lvl, f32dot(un["ta"], un["tb"]), 0.0)
    for un in units:
        e_g = jnp.exp(un["g_col"])
        rhs = jnp.concatenate([un["beta"] * un["vh"], un["beta"] * e_g * un["kn"]], axis=1).astype(BF16)
        un["uw"] = f32dot(un["t"].astype(BF16), rhs)
        un["qe"] = un["qn"] * e_g
    for un in units:
        s_old = s_ref[un["sq"], un["h"]]
        lhs = jnp.concatenate([un["uw"][:, B_HEAD_DIM:], un["qe"]], axis=0).astype(BF16)
        un["ws_qs"] = f32dot(lhs, s_old.astype(BF16))
    for un in units:
        sq, h = un["sq"], un["h"]
        wvb = (un["uw"][:, :B_HEAD_DIM] - un["ws_qs"][0:C]).astype(BF16)
        g_last = un["g_col"][C - 1:C, :]
        kd = (un["kn"] * jnp.exp(g_last - un["g_col"])).astype(BF16)
        o_ref[sq, :, h * B_HEAD_DIM:(h + 1) * B_HEAD_DIM] = un["ws_qs"][C:2 * C] + f32dot(un["qk"], wvb)
        s_ref[sq, h] = jnp.exp(g_last) * s_ref[sq, h] + _tn_dot(kd, wvb)


def _delta_rule(proj_main, conv0, s0, layer, conv_w, a_par, seq_len, chunk, nsb):
    rows, cols = proj_main.shape
    nseq = rows // seq_len
    nc = seq_len // chunk
    pm3 = proj_main.reshape(nseq, seq_len, cols)
    state_dims = (B_HEADS, B_HEAD_DIM, B_HEAD_DIM)
    o_b, s_new = pl.pallas_call(
        functools.partial(_delta_body, chunk=chunk, nsb=nsb),
        grid=(nseq // nsb, nc),
        in_specs=[pl.BlockSpec((nsb, chunk, CONV_CH), lambda n, c: (n, c, 0)),
                  pl.BlockSpec((nsb, chunk, BA_PAD), lambda n, c: (n, c, COL_BA)),
                  pl.BlockSpec((None, nsb, CONV_W - 1, CONV_CH), lambda n, c: (layer, n, 0, 0)),
                  pl.BlockSpec((None, nsb) + state_dims, lambda n, c: (layer, n, 0, 0, 0)),
                  pl.BlockSpec((CONV_W, CONV_CH), lambda n, c: (0, 0)),
                  pl.BlockSpec((2, BA_PAD), lambda n, c: (0, 0))],
        out_specs=[pl.BlockSpec((nsb, chunk, B_WIDTH), lambda n, c: (n, c, 0)),
                   pl.BlockSpec((nsb,) + state_dims, lambda n, c: (n, 0, 0, 0))],
        out_shape=[jax.ShapeDtypeStruct((nseq, seq_len, B_WIDTH), F32),
                   jax.ShapeDtypeStruct((nseq,) + state_dims, F32)],
        scratch_shapes=[pltpu.VMEM((nsb, chunk + 8, CONV_CH), F32)],
        compiler_params=_cparams("arbitrary", "arbitrary"),
        name="delta_c%d" % chunk,
    )(pm3, pm3, conv0, s0, conv_w, a_par)
    return o_b.reshape(rows, B_WIDTH), s_new


def _mix_out_body(x_ref, oa_ref, ob_ref, z_ref, ga_ref, gb_ref, g1_ref, sc2_ref, sh2_ref,
                  gn_ref, nf_ref, wa_ref, wb_ref, wo_ref, xo_ref, h2_ref, *, token_major):
    oa = jnp.concatenate([oa_ref[sl] for sl in range(SLABS)], axis=1)
    ya = jnp.dot(oa.astype(BF16), wa_ref[...], preferred_element_type=F32)
    gn = gn_ref[...]
    parts = []
    for h in range(B_HEADS):
        lo = h * B_HEAD_DIM
        t = ob_ref[:, lo:lo + B_HEAD_DIM]
        t = t * lax.rsqrt(jnp.mean(t * t, axis=-1, keepdims=True) + NORM_EPS) * gn
        parts.append((t * _silu(z_ref[:, lo:lo + B_HEAD_DIM])).astype(BF16))
    yb = jnp.dot(jnp.concatenate(parts, axis=1), wb_ref[...], preferred_element_type=F32)
    merged = _sigmoid(ga_ref[...]) * ya + _sigmoid(gb_ref[...]) * yb
    mix = jnp.dot(merged.astype(BF16), wo_ref[...], preferred_element_type=F32)
    xn = x_ref[...] + g1_ref[...] * mix
    xo_ref[...] = xn
    h2 = _rms_mod(xn, nf_ref[...], sc2_ref[...], sh2_ref[...])
    if token_major:
        for s in range(ROW_SLABS):
            h2_ref[pl.ds(s, xn.shape[0], stride=ROW_SLABS), :] = h2[:, s * LANES:(s + 1) * LANES]
    else:
        h2_ref[...] = h2.astype(BF16)


def _mixer_out(x, o_a, o_b, proj_main, g1, sc2, sh2, gdn_norm, norm_ffn, wa, wb, wo, *, tm, per_seq,
               token_major):
    rows, d = x.shape
    if token_major:
        h2_spec = pl.BlockSpec((tm * ROW_SLABS, LANES), lambda i: (i, 0))
        h2_shape = jax.ShapeDtypeStruct((rows * ROW_SLABS, LANES), F32)
    else:
        h2_spec = pl.BlockSpec((tm, d), lambda i: (i, 0))
        h2_shape = jax.ShapeDtypeStruct((rows, d), BF16)
    tps = (rows // g1.shape[0]) // tm if per_seq else None
    ms = _mod_spec(per_seq, tm, tps)

    def const(a):
        return pl.BlockSpec(a.shape, lambda i: (0,) * a.ndim)

    def colblk(c):
        return pl.BlockSpec((tm, d), lambda i: (i, c))

    return pl.pallas_call(
        functools.partial(_mix_out_body, token_major=token_major),
        grid=(rows // tm,),
        in_specs=[colblk(0), pl.BlockSpec((SLABS, tm, LANES), lambda i: (0, i, 0)), colblk(0),
                  colblk(COL_Z), colblk(COL_GA), colblk(COL_GB), ms, ms, ms,
                  const(gdn_norm), const(norm_ffn), const(wa), const(wb), const(wo)],
        out_specs=[colblk(0), h2_spec],
        out_shape=[jax.ShapeDtypeStruct((rows, d), F32), h2_shape],
        compiler_params=_cparams("arbitrary"),
        name="mixer_out",
    )(x, o_a, o_b, proj_main, proj_main, proj_main, g1, sc2, sh2, gdn_norm, norm_ffn, wa, wb, wo)


def _swiglu_accumulate(h, wg_ref, wu_ref, wd_ref, acc):
    gte = jnp.dot(h, wg_ref[...], preferred_element_type=F32)
    up = jnp.dot(h, wu_ref[...], preferred_element_type=F32)
    acc[...] += jnp.dot((_silu(gte) * up).astype(BF16), wd_ref[...], preferred_element_type=F32)


def _ffn_body(h_ref, base_ref, g2_ref, wg_ref, wu_ref, wd_ref, o_ref, acc):
    j = pl.program_id(1)

    @pl.when(j == 0)
    def _():
        acc[...] = jnp.zeros_like(acc)

    _swiglu_accumulate(h_ref[...], wg_ref, wu_ref, wd_ref, acc)

    @pl.when(j == pl.num_programs(1) - 1)
    def _():
        o_ref[...] = base_ref[...] + g2_ref[...] * acc[...]


def _ffn(h2, base, g2, w_up, w_down, *, tm, tf, per_seq):
    rows, d = h2.shape
    ff = w_down.shape[0]
    nj = ff // tf
    tps = (rows // g2.shape[0]) // tm if per_seq else None
    return pl.pallas_call(
        _ffn_body,
        grid=(rows // tm, nj),
        in_specs=[pl.BlockSpec((tm, d), lambda i, j: (i, 0)),
                  pl.BlockSpec((tm, d), lambda i, j: (i, 0)),
                  _mod_spec(per_seq, tm, tps),
                  pl.BlockSpec((d, tf), lambda i, j: (0, j)),
                  pl.BlockSpec((d, tf), lambda i, j: (0, j + nj)),
                  pl.BlockSpec((tf, d), lambda i, j: (j, 0))],
        out_specs=pl.BlockSpec((tm, d), lambda i, j: (i, 0)),
        out_shape=jax.ShapeDtypeStruct((rows, d), F32),
        scratch_shapes=[pltpu.VMEM((tm, d), F32)],
        compiler_params=_cparams("arbitrary", "arbitrary"),
        name="ffn",
    )(h2, base, g2, w_up, w_up, w_down)


R_I1, R_I2, R_W1, R_W2, R_R1, R_R2 = range(6)


def _router_body(x_ref, sc_ref, sh_ref, nf_ref, wr_ref, route_ref, cnt_ref, carry):
    @pl.when(pl.program_id(0) == 0)
    def _():
        carry[...] = jnp.zeros_like(carry)

    h = _rms_mod(x_ref[...], nf_ref[...], sc_ref[...], sh_ref[...])
    logits = jnp.dot(h, wr_ref[...], preferred_element_type=F32, precision=lax.Precision.HIGHEST)
    lane = lax.broadcasted_iota(jnp.int32, logits.shape, 1).astype(F32)
    width = float(logits.shape[1])
    logits = jnp.where(lane < N_EXPERTS, logits, NEG)
    m1 = jnp.max(logits, axis=1, keepdims=True)
    i1 = jnp.min(jnp.where(logits == m1, lane, width), axis=1, keepdims=True)
    rest = jnp.where(lane == i1, NEG, logits)
    m2 = jnp.max(rest, axis=1, keepdims=True)
    i2 = jnp.min(jnp.where(rest == m2, lane, width), axis=1, keepdims=True)
    e2 = jnp.exp(m2 - m1)
    w1 = 1.0 / (1.0 + e2)
    sel = jnp.where(lane == i1, 1.0, 0.0) + jnp.where(lane == i2, 1.0, 0.0)
    incl = _cumsum_rows(sel) + carry[...]
    excl = incl - sel
    r1 = jnp.sum(jnp.where(lane == i1, excl, 0.0), axis=1, keepdims=True)
    r2 = jnp.sum(jnp.where(lane == i2, excl, 0.0), axis=1, keepdims=True)
    total = incl[incl.shape[0] - 1:, :]
    carry[...] = total
    cnt_ref[...] = total
    route = jnp.zeros_like(logits)
    for k, v in ((R_I1, i1), (R_I2, i2), (R_W1, w1), (R_W2, e2 * w1), (R_R1, r1), (R_R2, r2)):
        route = jnp.where(lane == k, v, route)
    route_ref[...] = route


def _router(x, sc2, sh2, norm_ffn, w_router_pad, *, tm, per_seq):
    rows, d = x.shape
    tps = (rows // sc2.shape[0]) // tm if per_seq else None
    ms = _mod_spec(per_seq, tm, tps)
    width = w_router_pad.shape[1]
    return pl.pallas_call(
        _router_body,
        grid=(rows // tm,),
        in_specs=[pl.BlockSpec((tm, d), lambda i: (i, 0)), ms, ms,
                  pl.BlockSpec((1, d), lambda i: (0, 0)),
                  pl.BlockSpec((d, width), lambda i: (0, 0))],
        out_specs=[pl.BlockSpec((tm, width), lambda i: (i, 0)), pl.BlockSpec((1, width), lambda i: (0, 0))],
        out_shape=[jax.ShapeDtypeStruct((rows, width), F32), jax.ShapeDtypeStruct((1, width), F32)],
        scratch_shapes=[pltpu.VMEM((1, width), F32)],
        compiler_params=_cparams("arbitrary"),
        name="router",
    )(x, sc2, sh2, norm_ffn, w_router_pad)


def _route_plan(route, counts, tm):
    rows = route.shape[0]
    n_tiles = -(-2 * rows // tm) + N_EXPERTS + 1
    i1 = route[:, R_I1].astype(jnp.int32)
    i2 = route[:, R_I2].astype(jnp.int32)
    cnt = counts[0, :N_EXPERTS].astype(jnp.int32)
    tiles = (cnt + tm - 1) // tm
    ends = jnp.cumsum(tiles)
    starts = (ends - tiles) * tm
    pos1 = starts[i1] + route[:, R_R1].astype(jnp.int32)
    pos2 = starts[i2] + route[:, R_R2].astype(jnp.int32)
    tile_expert = jnp.minimum(jnp.sum(jnp.arange(n_tiles)[:, None] >= ends[None, :], axis=1), N_EXPERTS - 1)
    tok = jnp.arange(rows, dtype=jnp.int32)
    row_token = jnp.zeros((n_tiles * tm,), jnp.int32).at[jnp.concatenate([pos1, pos2])].set(
        jnp.concatenate([tok, tok]), unique_indices=True, mode="promise_in_bounds")
    return dict(n_tiles=n_tiles, tile_expert=tile_expert.astype(jnp.int32), n_used=ends[-1:].astype(jnp.int32),
                row_token=row_token.reshape(n_tiles, 1, tm), pos1=pos1, pos2=pos2)


def _start_row_gather(idx_ref, n, src_hbm, dst, sem):
    def body(r, carry):
        src = pl.multiple_of(idx_ref[0, 0, r] * ROW_SLABS, ROW_SLABS)
        pltpu.make_async_copy(src_hbm.at[pl.ds(src, ROW_SLABS), :],
                              dst.at[pl.ds(pl.multiple_of(r * ROW_SLABS, ROW_SLABS), ROW_SLABS), :], sem).start()
        return carry
    lax.fori_loop(0, n, body, 0, unroll=8)


def _start_row_gather_pair(idx_a, idx_b, n, src_hbm, dst_a, dst_b, sem_a, sem_b):
    def body(r, carry):
        row = pl.ds(pl.multiple_of(r * ROW_SLABS, ROW_SLABS), ROW_SLABS)
        for priority, (idx_ref, dst, sem) in enumerate(((idx_a, dst_a, sem_a), (idx_b, dst_b, sem_b))):
            src = pl.multiple_of(idx_ref[0, 0, r] * ROW_SLABS, ROW_SLABS)
            pltpu.make_async_copy(src_hbm.at[pl.ds(src, ROW_SLABS), :], dst.at[row, :], sem).start(priority=priority)
        return carry
    lax.fori_loop(0, n, body, 0, unroll=8)


def _wait_row_gather(n, src_hbm, dst, sem):
    pltpu.make_async_copy(src_hbm.at[pl.ds(0, n * ROW_SLABS), :], dst, sem).wait()


def _moe_ffn_body(te_ref, nu_ref, rt_ref, rtn_ref, h_hbm, wg_ref, wu_ref, wd_ref, o_ref,
                  xbuf, hbuf, acc, sem, *, tm, nj):
    i = pl.program_id(0)
    j = pl.program_id(1)
    used = i < nu_ref[0]
    slot = i % 2
    per_step = tm // nj

    @pl.when(jnp.logical_and(j == 0, i == 0))
    def _():
        _start_row_gather(rt_ref, tm, h_hbm, xbuf.at[0], sem.at[0])

    @pl.when(jnp.logical_and(j == 0, i <= nu_ref[0]))
    def _():
        _wait_row_gather(tm, h_hbm, xbuf.at[slot], sem.at[slot])

    @pl.when(jnp.logical_and(j == 0, used))
    def _():
        for s in range(ROW_SLABS):
            hbuf[:, s * LANES:(s + 1) * LANES] = xbuf[slot, pl.ds(s, tm, stride=ROW_SLABS), :].astype(BF16)
        acc[...] = jnp.zeros_like(acc)

    @pl.when(used)
    def _():
        for k in range(per_step):
            r = j * per_step + k
            src = pl.multiple_of(rtn_ref[0, 0, r] * ROW_SLABS, ROW_SLABS)
            pltpu.make_async_copy(
                h_hbm.at[pl.ds(src, ROW_SLABS), :],
                xbuf.at[1 - slot, pl.ds(pl.multiple_of(r * ROW_SLABS, ROW_SLABS), ROW_SLABS), :],
                sem.at[1 - slot]).start()
        _swiglu_accumulate(hbuf[...], wg_ref, wu_ref, wd_ref, acc)

    last = j == pl.num_programs(1) - 1

    @pl.when(jnp.logical_and(last, used))
    def _():
        for s in range(ROW_SLABS):
            o_ref[pl.ds(s, tm, stride=ROW_SLABS), :] = acc[:, s * LANES:(s + 1) * LANES]

    @pl.when(jnp.logical_and(last, jnp.logical_not(used)))
    def _():
        o_ref[...] = jnp.zeros_like(o_ref)


def _moe_ffn(h_rows, plan, w_up, w_down, moe_idx, *, tm, tf):
    d = w_up.shape[-2]
    ff = w_down.shape[-2]
    nj = ff // tf
    n_tiles = plan["n_tiles"]

    def jj(i, j, nu):
        return jnp.where(i < nu[0], j, nj - 1)

    grid_spec = pltpu.PrefetchScalarGridSpec(
        num_scalar_prefetch=2,
        grid=(n_tiles, nj),
        in_specs=[pl.BlockSpec((1, 1, tm), lambda i, j, te, nu: (i, 0, 0), memory_space=pltpu.SMEM),
                  pl.BlockSpec((1, 1, tm), lambda i, j, te, nu: (jnp.minimum(i + 1, n_tiles - 1), 0, 0),
                               memory_space=pltpu.SMEM),
                  pl.BlockSpec(memory_space=pl.ANY),
                  pl.BlockSpec((None, None, d, tf), lambda i, j, te, nu: (moe_idx, te[i], 0, jj(i, j, nu))),
                  pl.BlockSpec((None, None, d, tf), lambda i, j, te, nu: (moe_idx, te[i], 0, jj(i, j, nu) + nj)),
                  pl.BlockSpec((None, None, tf, d), lambda i, j, te, nu: (moe_idx, te[i], jj(i, j, nu), 0))],
        out_specs=pl.BlockSpec((tm * ROW_SLABS, LANES), lambda i, j, te, nu: (i, 0)),
        scratch_shapes=[pltpu.VMEM((2, tm * ROW_SLABS, LANES), F32), pltpu.VMEM((tm, d), BF16),
                        pltpu.VMEM((tm, d), F32), pltpu.SemaphoreType.DMA((2,))])
    return pl.pallas_call(
        functools.partial(_moe_ffn_body, tm=tm, nj=nj),
        grid_spec=grid_spec,
        out_shape=jax.ShapeDtypeStruct((n_tiles * tm * ROW_SLABS, LANES), F32),
        compiler_params=pltpu.CompilerParams(dimension_semantics=("arbitrary", "arbitrary"),
                                             vmem_limit_bytes=VMEM_LIMIT, disable_bounds_checks=True),
        name="moe_ffn",
    )(plan["tile_expert"], plan["n_used"], plan["row_token"], plan["row_token"], h_rows, w_up, w_up, w_down)


def _moe_combine_body(p1_ref, p2_ref, p1n_ref, p2n_ref, y_hbm, x_ref, g2_ref, route_ref, fw_ref, o_ref,
                      buf1, buf2, sem, *, tc, final):
    i = pl.program_id(0)
    slot = i % 2

    @pl.when(i == 0)
    def _():
        _start_row_gather_pair(p1_ref, p2_ref, tc, y_hbm, buf1.at[0], buf2.at[0], sem.at[0, 0], sem.at[1, 0])

    _wait_row_gather(tc, y_hbm, buf1.at[slot], sem.at[0, slot])
    _wait_row_gather(tc, y_hbm, buf2.at[slot], sem.at[1, slot])

    @pl.when(i + 1 < pl.num_programs(0))
    def _():
        _start_row_gather_pair(p1n_ref, p2n_ref, tc, y_hbm, buf1.at[1 - slot], buf2.at[1 - slot],
                               sem.at[0, 1 - slot], sem.at[1, 1 - slot])

    w1 = route_ref[:, R_W1:R_W1 + 1]
    w2 = route_ref[:, R_W2:R_W2 + 1]
    pieces = []
    for s in range(ROW_SLABS):
        cols = slice(s * LANES, (s + 1) * LANES)
        y = (w1 * buf1[slot, pl.ds(s, tc, stride=ROW_SLABS), :]
             + w2 * buf2[slot, pl.ds(s, tc, stride=ROW_SLABS), :])
        pieces.append(x_ref[:, cols] + g2_ref[:, cols] * y)
    xn = jnp.concatenate(pieces, axis=1)
    if final:
        xn = xn * lax.rsqrt(jnp.mean(xn * xn, axis=-1, keepdims=True) + NORM_EPS) * fw_ref[...]
    o_ref[...] = xn


def _moe_combine(y_rows, plan, route, x, g2, final_w, *, tc, per_seq, final):
    rows, d = x.shape
    nt = rows // tc
    tps = (rows // g2.shape[0]) // tc if per_seq else None
    p1 = plan["pos1"].reshape(nt, 1, tc)
    p2 = plan["pos2"].reshape(nt, 1, tc)
    cur = pl.BlockSpec((1, 1, tc), lambda i: (i, 0, 0), memory_space=pltpu.SMEM)
    nxt = pl.BlockSpec((1, 1, tc), lambda i: (jnp.minimum(i + 1, nt - 1), 0, 0), memory_space=pltpu.SMEM)
    return pl.pallas_call(
        functools.partial(_moe_combine_body, tc=tc, final=final),
        grid=(nt,),
        in_specs=[cur, cur, nxt, nxt, pl.BlockSpec(memory_space=pl.ANY),
                  pl.BlockSpec((tc, d), lambda i: (i, 0)), _mod_spec(per_seq, tc, tps),
                  pl.BlockSpec((tc, route.shape[1]), lambda i: (i, 0)),
                  pl.BlockSpec((1, d), lambda i: (0, 0))],
        out_specs=pl.BlockSpec((tc, d), lambda i: (i, 0)),
        out_shape=jax.ShapeDtypeStruct((rows, d), F32),
        scratch_shapes=[pltpu.VMEM((2, tc * ROW_SLABS, LANES), F32), pltpu.VMEM((2, tc * ROW_SLABS, LANES), F32),
                        pltpu.SemaphoreType.DMA((2, 2))],
        compiler_params=pltpu.CompilerParams(dimension_semantics=("arbitrary",),
                                             vmem_limit_bytes=VMEM_LIMIT, disable_bounds_checks=True),
        name="moe_combine",
    )(p1, p2, p1, p2, y_rows, x, g2, route, final_w)


def _final_body(x_ref, w_ref, o_ref):
    x = x_ref[...]
    o_ref[...] = x * lax.rsqrt(jnp.mean(x * x, axis=-1, keepdims=True) + NORM_EPS) * w_ref[...]


def _final_norm(x, w, *, tm):
    rows, d = x.shape
    return pl.pallas_call(
        _final_body,
        grid=(rows // tm,),
        in_specs=[pl.BlockSpec((tm, d), lambda i: (i, 0)), pl.BlockSpec((1, d), lambda i: (0, 0))],
        out_specs=pl.BlockSpec((tm, d), lambda i: (i, 0)),
        out_shape=jax.ShapeDtypeStruct((rows, d), F32),
        compiler_params=_cparams("arbitrary"),
        name="final_norm",
    )(x, w)


def _rope_tables(pos):
    half = A_HEAD_DIM // 2
    inv = jnp.exp(-math.log(ROPE_THETA) * jnp.arange(half, dtype=F32) / half)
    ang = pos.astype(F32)[:, None] * inv[None, :]
    cos, sin = jnp.cos(ang), jnp.sin(ang)
    cos_t = jnp.tile(jnp.concatenate([cos, cos], axis=1), (1, HEADS_PER_SLAB))
    sin_t = jnp.tile(jnp.concatenate([-sin, sin], axis=1), (1, HEADS_PER_SLAB))
    return cos_t, sin_t


def _slab_rows(pa, c, nseq, seq_len, keep):
    if keep == seq_len:
        t = pa[c * SLABS:(c + 1) * SLABS]
    else:
        t = jnp.stack([lax.slice(pa, (c * SLABS, n * seq_len + seq_len - keep, 0),
                                 ((c + 1) * SLABS, (n + 1) * seq_len, LANES)) for n in range(nseq)], axis=1)
    t = t.reshape(SLABS, nseq, keep, HEADS_PER_SLAB, A_HEAD_DIM)
    return t.transpose(1, 2, 0, 3, 4).reshape(nseq, keep, A_HEADS, A_HEAD_DIM)


def _split_w_in(w):
    o_b = A_QKV
    o_z = o_b + CONV_CH
    o_ba = o_z + B_WIDTH
    o_g = o_ba + 2 * B_HEADS
    pad = jnp.zeros((w.shape[0], BA_PAD - 2 * B_HEADS), w.dtype)
    main = jnp.concatenate([w[:, o_b:o_ba], w[:, o_g:], w[:, o_ba:o_g], pad], axis=1)
    return w[:, :A_QKV].astype(BF16), main.astype(BF16)


def kernel(x_prompt, x_sample, cache_k_g0, cache_v_g0, cache_k_g1, cache_v_g1, cache_k_g2, cache_v_g2,
           state_conv, state_delta, c_prompt, c_sample, w_mod, b_mod, norm_mix, norm_ffn, w_in, conv_w,
           a_log, dt_bias, gdn_norm, w_branch_a, w_branch_b, w_out, w_ffn_up, w_ffn_down, w_router,
           w_exp_up, w_exp_down, final_norm):
    depth = w_in.shape[0]
    nb, seq, d = x_prompt.shape
    ns, dec, _ = x_sample.shape
    past = cache_k_g2.shape[2]

    c_all = jnp.concatenate([c_prompt, c_sample, jnp.zeros((-(nb + ns) % 8, d), F32)], axis=0)
    mod_all = _modulation(c_all, w_mod, b_mod)

    caches = [_window_minor(c) for c in (cache_k_g0, cache_v_g0, cache_k_g1, cache_v_g1, cache_k_g2, cache_v_g2)]
    tabs_p = _rope_tables(jnp.tile(jnp.arange(seq), nb))
    tabs_s = _rope_tables(jnp.tile(past + jnp.arange(dec), ns))
    zero_conv = jnp.zeros((depth, nb, CONV_W - 1, CONV_CH), F32)
    zero_state = jnp.zeros((depth, nb, B_HEADS, B_HEAD_DIM, B_HEAD_DIM), F32)

    exp_up = w_exp_up.astype(BF16)
    exp_down = w_exp_down.astype(BF16)
    xp = x_prompt.reshape(nb * seq, d)
    xs = x_sample.reshape(ns * dec, d)
    new_p = [[] for _ in range(8)]
    new_s = [[] for _ in range(8)]
    tm_p, tm_s = 512, 512

    for l in range(depth):
        mods_p = [m.reshape(nb, 1, d) for m in jnp.split(mod_all[l, :nb], 6, axis=-1)]
        mods_s = [jnp.repeat(m, dec, axis=0) for m in jnp.split(mod_all[l, nb:nb + ns], 6, axis=-1)]
        w_attn, w_main = _split_w_in(w_in[l])
        nw_mix = norm_mix[l].reshape(1, d)
        nw_ffn = norm_ffn[l].reshape(1, d)
        a_par = jnp.zeros((2, BA_PAD), F32)
        a_par = a_par.at[0, B_HEADS:2 * B_HEADS].set(a_log[l]).at[1, B_HEADS:2 * B_HEADS].set(dt_bias[l])
        wa = w_branch_a[l].astype(BF16)
        wb = w_branch_b[l].astype(BF16)
        wo = w_out[l].astype(BF16)
        gn = gdn_norm[l].reshape(1, B_HEAD_DIM)

        outs = []
        for (x, mods, per_seq, tm, tabs, is_prompt) in ((xp, mods_p, True, tm_p, tabs_p, True),
                                                        (xs, mods_s, False, tm_s, tabs_s, False)):
            sh1, sc1, g1, sh2, sc2, g2 = mods
            tm_proj = min(x.shape[0], PROJ_ROWS)
            pm = _project(x, sc1, sh1, nw_mix, w_main, tm=tm_proj, tn=MAIN_COLS // 5, per_seq=per_seq,
                          name="proj_main")
            pa = _project(x, sc1, sh1, nw_mix, w_attn, tm=tm_proj, tn=3 * A_WIDTH, per_seq=per_seq,
                          rope_tabs=tabs, name="proj_attn")
            if is_prompt:
                run = None
                for gi in range(N_GROUPS):
                    run = _band_attention(pa, gi, seq, run)
                o_a = run[0]
                o_b, s_new = _delta_rule(pm, zero_conv, zero_state, l, conv_w[l], a_par, seq, PROMPT_CHUNK, nb)
            else:
                o_a = _cached_attention(pa, caches, l, dec)
                o_b, s_new = _delta_rule(pm, state_conv, state_delta, l, conv_w[l], a_par, dec, dec, SAMPLE_SEQS_PER_STEP)
            x, h2 = _mixer_out(x, o_a, o_b, pm, g1, sc2, sh2, gn, nw_ffn, wa, wb, wo, tm=tm, per_seq=per_seq,
                               token_major=l % 2 == 1)
            if l % 2 == 0:
                x = _ffn(h2, x, g2, w_ffn_up[l // 2].astype(BF16), w_ffn_down[l // 2].astype(BF16),
                         tm=tm, tf=1408, per_seq=per_seq)
            else:
                wr = jnp.zeros((d, LANES), F32).at[:, :N_EXPERTS].set(w_router[l // 2])
                route, counts = _router(x, sc2, sh2, nw_ffn, wr, tm=tm, per_seq=per_seq)
                tm_moe = MOE_ROWS if x.shape[0] >= 8 * MOE_ROWS else MOE_ROWS // 2
                plan = _route_plan(route, counts, tm_moe)
                y_rows = _moe_ffn(h2, plan, exp_up, exp_down, l // 2, tm=tm_moe, tf=1792)
                x = _moe_combine(y_rows, plan, route, x, g2, final_norm.reshape(1, d), tc=min(512, x.shape[0]),
                                 per_seq=per_seq, final=l == depth - 1)
            outs.append((x, pa, pm, s_new))

        (xp, pa_p, pm_p, s_p), (xs, pa_s, pm_s, s_s) = outs
        for gi, (win, dil) in enumerate(ATTN_GROUPS):
            keep = min(win, seq)
            new_p[2 * gi].append(_slab_rows(pa_p, 3 * gi + 1, nb, seq, keep))
            new_p[2 * gi + 1].append(_slab_rows(pa_p, 3 * gi + 2, nb, seq, keep))
            new_s[2 * gi].append(_slab_rows(pa_s, 3 * gi + 1, ns, dec, dec))
            new_s[2 * gi + 1].append(_slab_rows(pa_s, 3 * gi + 2, ns, dec, dec))
        new_p[6].append(pm_p.reshape(nb, seq, MAIN_COLS)[:, seq - (CONV_W - 1):, :CONV_CH])
        new_p[7].append(s_p)
        conv_s = jnp.concatenate([state_conv[l], pm_s.reshape(ns, dec, MAIN_COLS)[:, :, :CONV_CH]], axis=1)
        new_s[6].append(conv_s[:, -(CONV_W - 1):])
        new_s[7].append(s_s)

    if depth % 2 == 1:
        xp = _final_norm(xp, final_norm.reshape(1, d), tm=tm_p)
        xs = _final_norm(xs, final_norm.reshape(1, d), tm=tm_s)
    y_prompt = xp.reshape(nb, seq, d)
    y_sample = xs.reshape(ns, dec, d)
    return (y_prompt, y_sample, *[jnp.stack(t) for t in new_p], *[jnp.stack(t) for t in new_s])
```
